```python
import jax, jax.numpy as jnp
from jax import lax
import numpy as np

D_MODEL = 1024
BATCH = 16
SEQ = 2048
DEPTH = 2

HEAD_DIM = 64
DIL_GROUPS = ((128, 1), (512, 4), (2048, 16))
DIL_HEADS_PER_GROUP = 4
DIL_HEADS = DIL_HEADS_PER_GROUP * len(DIL_GROUPS)
MOBA_HEADS = 4
MOBA_BLOCK = 256
MOBA_TOPK = 3
Q_CHUNK = 128
N_ATTN_HEADS = DIL_HEADS + MOBA_HEADS
DIL_WIDTH = DIL_HEADS * HEAD_DIM
MOBA_WIDTH = MOBA_HEADS * HEAD_DIM
DIL_OUT = DIL_HEADS_PER_GROUP * HEAD_DIM
PROJ_COLS = 3 * DIL_WIDTH + 3 * MOBA_WIDTH + 2 * D_MODEL
PROJ_SPLITS = (DIL_WIDTH, 2 * DIL_WIDTH, 3 * DIL_WIDTH,
               3 * DIL_WIDTH + MOBA_WIDTH, 3 * DIL_WIDTH + 2 * MOBA_WIDTH,
               3 * DIL_WIDTH + 3 * MOBA_WIDTH, 3 * DIL_WIDTH + 3 * MOBA_WIDTH + D_MODEL)
D_FF = 2816
CONV_WIDTH = 3
RMS_EPS = 1e-6
SCALE = HEAD_DIM ** -0.5

kernel_name = 'hybrid_dilated_moba_convffn_block'


def rms_norm(x, g):
    xf = x.astype(jnp.float32)
    y = xf * lax.rsqrt(jnp.mean(xf * xf, axis=-1, keepdims=True) + RMS_EPS)
    return (y * g.astype(jnp.float32)).astype(x.dtype)


def alibi_slopes():
    i = jnp.arange(1, N_ATTN_HEADS + 1, dtype=jnp.float32)
    return jnp.exp2(-8.0 * i / N_ATTN_HEADS)


def dilated_attention(q, k, v, window, dilation, slopes):
    B, S, H, hd = q.shape
    d = dilation
    steps = window // d
    blk = steps
    L = S // d
    Lp = -(-L // blk) * blk
    nblk = Lp // blk

    def to_blocks(a):
        a = a.reshape(B, L, d, H, hd)
        a = jnp.pad(a, ((0, 0), (0, Lp - L), (0, 0), (0, 0), (0, 0)))
        return a.reshape(B, nblk, blk, d, H, hd)

    def with_prev(a):
        prev = jnp.concatenate([jnp.zeros_like(a[:, :1]), a[:, :-1]], axis=1)
        return jnp.concatenate([prev, a], axis=2)

    qb = to_blocks(q)
    kc = with_prev(to_blocks(k))
    vc = with_prev(to_blocks(v))
    s = jnp.einsum('bnqrhd,bnkrhd->bnrhqk', qb, kc).astype(jnp.float32) * SCALE
    qi = jnp.arange(blk)
    ki = jnp.arange(2 * blk)
    delta = blk + qi[:, None] - ki[None, :]
    kstep = jnp.arange(nblk)[:, None, None] * blk - blk + ki[None, None, :]
    valid = (delta >= 0) & (delta <= steps) & (kstep >= 0)
    bias = -(slopes.astype(jnp.float32) * d)[:, None, None] * delta.astype(jnp.float32)
    s = jnp.where(valid[None, :, None, None], s + bias[None, None, None], -jnp.inf)
    lse = jax.nn.logsumexp(s, axis=-1)
    p = jnp.exp(s - lse[..., None]).astype(v.dtype)
    o = jnp.einsum('bnrhqk,bnkrhd->bnqrhd', p, vc)
    o = o.reshape(B, Lp, d, H, hd)[:, :L].reshape(B, S, H, hd)
    lse = lse.transpose(0, 1, 4, 2, 3).reshape(B, Lp, d, H)[:, :L].reshape(B, S, H)
    return o, lse


def moba_attention(q, k, v, slopes):
    B, S, H, hd = q.shape
    Sp = -(-S // MOBA_BLOCK) * MOBA_BLOCK
    nb = Sp // MOBA_BLOCK
    k_sel = min(MOBA_TOPK, nb - 1)
    pad = ((0, 0), (0, Sp - S), (0, 0), (0, 0))
    kblk = jnp.pad(k, pad).reshape(B, nb, MOBA_BLOCK, H, hd).transpose(0, 3, 1, 2, 4)
    vblk = jnp.pad(v, pad).reshape(B, nb, MOBA_BLOCK, H, hd).transpose(0, 3, 1, 2, 4)
    nq = S // Q_CHUNK
    slopes = slopes.astype(jnp.float32)
    q_chunks = q.reshape(B, nq, Q_CHUNK, H, hd).transpose(0, 1, 3, 2, 4).reshape(B * nq, H, Q_CHUNK, hd)
    b_ids = jnp.repeat(jnp.arange(B, dtype=jnp.int32), nq)
    c_ids = jnp.tile(jnp.arange(nq, dtype=jnp.int32), B)
    xs = (q_chunks, b_ids, c_ids)
    if k_sel > 0:
        pos = jnp.arange(S)
        own_blk = pos // MOBA_BLOCK
        kmean = jnp.mean(kblk.astype(jnp.float32), axis=3)
        gate = jnp.einsum('bshd,bhnd->bhsn', q.astype(jnp.float32), kmean)
        past = jnp.arange(nb)[None, :] < own_blk[:, None]
        gate = jnp.where(past[None, None], gate, -jnp.inf)
        _, gidx = lax.top_k(gate, k_sel)
        gvalid = gidx < own_blk[None, None, :, None]
        idx_chunks = gidx.reshape(B, H, nq, Q_CHUNK, k_sel).transpose(0, 2, 1, 3, 4).reshape(B * nq, H, Q_CHUNK, k_sel)
        val_chunks = gvalid.reshape(B, H, nq, Q_CHUNK, k_sel).transpose(0, 2, 1, 3, 4).reshape(B * nq, H, Q_CHUNK, k_sel)
        xs = xs + (idx_chunks, val_chunks)

    def body(args):
        qc, b, c = args[0], args[1], args[2]
        kb = kblk[b]
        vb = vblk[b]
        tq = c * Q_CHUNK + jnp.arange(Q_CHUNK)
        own = (c * Q_CHUNK) // MOBA_BLOCK
        k_own = lax.dynamic_index_in_dim(kb, own, axis=1, keepdims=False)
        v_own = lax.dynamic_index_in_dim(vb, own, axis=1, keepdims=False)
        tk_own = own * MOBA_BLOCK + jnp.arange(MOBA_BLOCK)
        dist_own = (tq[:, None] - tk_own[None, :]).astype(jnp.float32)
        s_own = jnp.einsum('hqd,hkd->hqk', qc, k_own).astype(jnp.float32) * SCALE - slopes[:, None, None] * dist_own
        s_own = jnp.where((dist_own >= 0)[None], s_own, -jnp.inf)
        if k_sel > 0:
            ic, vm = args[3], args[4]
            k_s = jax.vmap(lambda kh, ih: kh[ih])(kb, ic)
            v_s = jax.vmap(lambda vh, ih: vh[ih])(vb, ic)
            tk_s = ic[..., None] * MOBA_BLOCK + jnp.arange(MOBA_BLOCK)
            dist_s = (tq[None, :, None, None] - tk_s).astype(jnp.float32)
            s_s = jnp.einsum('hqd,hqknd->hqkn', qc, k_s).astype(jnp.float32) * SCALE - slopes[:, None, None, None] * dist_s
            s_s = jnp.where(vm[..., None], s_s, -jnp.inf).reshape(H, Q_CHUNK, k_sel * MOBA_BLOCK)
            p = jax.nn.softmax(jnp.concatenate([s_s, s_own], axis=-1), axis=-1).astype(v.dtype)
            n_s = k_sel * MOBA_BLOCK
            o = (jnp.einsum('hqn,hqnd->hqd', p[..., :n_s], v_s.reshape(H, Q_CHUNK, n_s, hd))
                 + jnp.einsum('hqk,hkd->hqd', p[..., n_s:], v_own))
        else:
            p = jax.nn.softmax(s_own, axis=-1).astype(v.dtype)
            o = jnp.einsum('hqk,hkd->hqd', p, v_own)
        return o.astype(v.dtype)

    out = lax.map(body, xs)
    return out.reshape(B, nq, H, Q_CHUNK, hd).transpose(0, 1, 3, 2, 4).reshape(B, S, H, hd)


def mixer_sublayer(x, g_pre, g_post, w_in, w_br_dil, w_br_moba, w_out):
    B, S, _ = x.shape
    h = rms_norm(x, g_pre)
    proj = h @ w_in
    qa, ka, va, qb, kb, vb, gate_a, gate_b = jnp.split(proj, PROJ_SPLITS, axis=-1)
    qa = qa.reshape(B, S, DIL_HEADS, HEAD_DIM)
    ka = ka.reshape(B, S, DIL_HEADS, HEAD_DIM)
    va = va.reshape(B, S, DIL_HEADS, HEAD_DIM)
    qb = qb.reshape(B, S, MOBA_HEADS, HEAD_DIM)
    kb = kb.reshape(B, S, MOBA_HEADS, HEAD_DIM)
    vb = vb.reshape(B, S, MOBA_HEADS, HEAD_DIM)
    slopes = alibi_slopes()
    outs, lses = [], []
    for g, (window, dilation) in enumerate(DIL_GROUPS):
        sl = slice(g * DIL_HEADS_PER_GROUP, (g + 1) * DIL_HEADS_PER_GROUP)
        o, lse = dilated_attention(qa[:, :, sl], ka[:, :, sl], va[:, :, sl], window, dilation, slopes[sl])
        outs.append(o)
        lses.append(lse)
    alpha = jax.nn.softmax(jnp.stack(lses, axis=0), axis=0)
    o_a = jnp.sum(alpha[..., None] * jnp.stack(outs, axis=0).astype(jnp.float32), axis=0).astype(x.dtype)
    o_b = moba_attention(qb, kb, vb, slopes[DIL_HEADS:])
    y_a = o_a.reshape(B, S, DIL_OUT) @ w_br_dil
    y_b = o_b.reshape(B, S, MOBA_WIDTH) @ w_br_moba
    merged = jax.nn.sigmoid(gate_a) * y_a + jax.nn.sigmoid(gate_b) * y_b
    return x + rms_norm(merged @ w_out, g_post)


def ffn_sublayer(x, g_pre, g_post, w_gate, w_up, conv_w, conv_b, w_down):
    h = rms_norm(x, g_pre)
    a = h @ w_gate
    a = lax.conv_general_dilated(a, conv_w.astype(a.dtype)[:, None, :], window_strides=(1,),
                                 padding=[(CONV_WIDTH - 1, 0)],
                                 dimension_numbers=('NWC', 'WIO', 'NWC'),
                                 feature_group_count=D_FF) + conv_b
    u = jax.nn.gelu(a, approximate=True) * (h @ w_up)
    return x + rms_norm(u @ w_down, g_post)


def setup_inputs(seed: int = 0) -> dict:
    key = jax.random.key(seed)
    ks = jax.random.split(key, 14)
    f32 = jnp.float32

    def nrm(k, shape, fan_in):
        return jax.random.normal(k, shape, f32) * (fan_in ** -0.5)

    def gain(k, n):
        return 1.0 + 0.05 * jax.random.normal(k, (DEPTH, n), f32)

    return {
        'x': jax.random.normal(ks[0], (BATCH, SEQ, D_MODEL), f32),
        'mix_norm_pre': gain(ks[1], D_MODEL),
        'mix_norm_post': gain(ks[2], D_MODEL),
        'w_in': nrm(ks[3], (DEPTH, D_MODEL, PROJ_COLS), D_MODEL),
        'w_branch_dil': nrm(ks[4], (DEPTH, DIL_OUT, D_MODEL), DIL_OUT),
        'w_branch_moba': nrm(ks[5], (DEPTH, MOBA_WIDTH, D_MODEL), MOBA_WIDTH),
        'w_out': nrm(ks[6], (DEPTH, D_MODEL, D_MODEL), D_MODEL),
        'ffn_norm_pre': gain(ks[7], D_MODEL),
        'ffn_norm_post': gain(ks[8], D_MODEL),
        'w_ffn_gate': nrm(ks[9], (DEPTH, D_MODEL, D_FF), D_MODEL),
        'w_ffn_up': nrm(ks[10], (DEPTH, D_MODEL, D_FF), D_MODEL),
        'ffn_conv_w': nrm(ks[11], (DEPTH, CONV_WIDTH, D_FF), CONV_WIDTH),
        'ffn_conv_b': 0.02 * jax.random.normal(ks[12], (DEPTH, D_FF), f32),
        'w_ffn_down': nrm(ks[13], (DEPTH, D_FF, D_MODEL), D_FF),
    }


def reference(x, mix_norm_pre, mix_norm_post, w_in, w_branch_dil, w_branch_moba, w_out,
              ffn_norm_pre, ffn_norm_post, w_ffn_gate, w_ffn_up, ffn_conv_w, ffn_conv_b, w_ffn_down):
    for l in range(DEPTH):
        x = mixer_sublayer(x, mix_norm_pre[l], mix_norm_post[l], w_in[l],
                           w_branch_dil[l], w_branch_moba[l], w_out[l])
        x = ffn_sublayer(x, ffn_norm_pre[l], ffn_norm_post[l], w_ffn_gate[l], w_ffn_up[l],
                         ffn_conv_w[l], ffn_conv_b[l], w_ffn_down[l])
    return x
```

```python
import functools

import jax
import jax.numpy as jnp
from jax import lax
from jax.experimental import pallas as pl
from jax.experimental.pallas import tpu as pltpu

F32 = jnp.float32
BF16 = jnp.bfloat16

HEAD_DIM = 64
DIL_GROUPS = ((128, 1), (512, 4), (2048, 16))
DIL_HEADS_PER_GROUP = 4
DIL_HEADS = DIL_HEADS_PER_GROUP * len(DIL_GROUPS)
MOBA_HEADS = 4
MOBA_BLOCK = 256
MOBA_TOPK = 3
N_ATTN_HEADS = DIL_HEADS + MOBA_HEADS
DIL_WIDTH = DIL_HEADS * HEAD_DIM
MOBA_WIDTH = MOBA_HEADS * HEAD_DIM
DIL_OUT = DIL_HEADS_PER_GROUP * HEAD_DIM
DIL_STEPS = 128
RMS_EPS = 1e-6
SCALE = HEAD_DIM ** -0.5
CONV_WIDTH = 3

LANES = 128
HEADS_PER_TILE = LANES // HEAD_DIM
MASKED = -(2.0 ** 100)
VMEM_LIMIT = 56 * 1024 * 1024

_NT = (((1,), (1,)), ((), ()))


def _rms_norm(x, g):
    return x * lax.rsqrt(jnp.mean(x * x, axis=-1, keepdims=True) + RMS_EPS) * g


def _resident(shape):
    return pl.BlockSpec(shape, lambda *_: (0,) * len(shape), pipeline_mode=pl.Buffered(1))


def _inproj_kernel(x_ref, g_ref, wd_ref, wm_ref, wg_ref, dil_ref, moba_ref, gate_ref):
    h = _rms_norm(x_ref[...], g_ref[...]).astype(BF16)
    for w_ref, o_ref, chunk in ((wd_ref, dil_ref, 768), (wm_ref, moba_ref, 768), (wg_ref, gate_ref, 512)):
        for c in range(0, w_ref.shape[1], chunk):
            o_ref[:, c:c + chunk] = jnp.dot(
                h, w_ref[:, c:c + chunk], preferred_element_type=F32).astype(o_ref.dtype)


def _inproj(x, g, w_dil, w_moba, w_gate, tm):
    n, d = x.shape
    widths = (w_dil.shape[1], w_moba.shape[1], w_gate.shape[1])
    return pl.pallas_call(
        _inproj_kernel,
        grid=(n // tm,),
        in_specs=[pl.BlockSpec((tm, d), lambda i: (i, 0)), _resident((1, d)),
                  _resident(w_dil.shape), _resident(w_moba.shape), _resident(w_gate.shape)],
        out_specs=[pl.BlockSpec((tm, w), lambda i: (i, 0)) for w in widths],
        out_shape=[jax.ShapeDtypeStruct((n, w), BF16) for w in widths],
        compiler_params=pltpu.CompilerParams(dimension_semantics=("parallel",),
                                             vmem_limit_bytes=VMEM_LIMIT),
        name="inproj",
    )(x, g, w_dil, w_moba, w_gate)


def _dil_kernel(slopes_ref, q_ref, k_ref, v_ref, o_ref, lse_ref, qm_ref, bias_ref, *, dilation, head0):
    jp = pl.program_id(1)
    sub_len = q_ref.shape[1]
    blk = DIL_STEPS
    nblk = sub_len // blk

    lane = lax.broadcasted_iota(jnp.int32, (sub_len, LANES), 1)
    q2 = q_ref[0]
    qi = lax.broadcasted_iota(jnp.int32, (blk, 2 * blk), 0)
    ki = lax.broadcasted_iota(jnp.int32, (blk, 2 * blk), 1)
    delta = blk + qi - ki
    valid = (delta >= 0) & (delta <= DIL_STEPS)
    for hh in range(HEADS_PER_TILE):
        in_half = (lane >= hh * HEAD_DIM) & (lane < (hh + 1) * HEAD_DIM)
        qm_ref[hh] = jnp.where(in_half, q2, jnp.zeros_like(q2)) * jnp.asarray(SCALE, BF16)
        slope = slopes_ref[head0 + HEADS_PER_TILE * jp + hh] * float(dilation)
        bias_ref[hh] = jnp.where(valid, -slope * delta.astype(F32), MASKED)

    first_half = lax.broadcasted_iota(jnp.int32, (blk, LANES), 1) < HEAD_DIM

    def block(row0, has_prev):
        q_rows = pl.ds(row0, blk)
        kv_rows = pl.ds(row0 - blk, 2 * blk) if has_prev else pl.ds(row0, blk)
        k2 = k_ref[0, kv_rows, :]
        v2 = v_ref[0, kv_rows, :]
        outs, lses = [], []
        for hh in range(HEADS_PER_TILE):
            s = lax.dot_general(qm_ref[hh, q_rows, :], k2, _NT, preferred_element_type=F32)
            s = s + (bias_ref[hh] if has_prev else bias_ref[hh, :, blk:])
            m = jnp.max(s, axis=-1, keepdims=True)
            p = jnp.exp(s - m)
            l = jnp.sum(p, axis=-1, keepdims=True)
            o = jnp.dot(p.astype(BF16), v2, preferred_element_type=F32)
            outs.append(o / l)
            lses.append(jnp.broadcast_to(m + jnp.log(l), (blk, LANES)))
        o_ref[0, q_rows, :] = jnp.where(first_half, outs[0], outs[1]).astype(o_ref.dtype)
        lse_ref[0, q_rows, :] = jnp.where(first_half, lses[0], lses[1])

    block(0, False)
    if nblk > 1:
        def body(n, carry):
            block(pl.multiple_of(n * blk, blk), True)
            return carry
        lax.fori_loop(1, nblk, body, 0)


def _dilated_attention(slopes, qkv, batch, seq, group):
    _, dilation = DIL_GROUPS[group]
    sub_len = seq // dilation
    cols = qkv.shape[-1]
    tiles_per_row = cols // LANES
    tiles_per_part = DIL_WIDTH // LANES
    tiles_per_group = DIL_HEADS_PER_GROUP // HEADS_PER_TILE
    view = qkv.reshape(batch, sub_len, dilation * cols)

    def in_map(part):
        def index(b, jp, r):
            return (b, 0, r * tiles_per_row + part * tiles_per_part + group * tiles_per_group + jp)
        return index

    def out_map(b, jp, r):
        return (b, 0, r * tiles_per_group + jp)

    blk_spec = lambda imap: pl.BlockSpec((1, sub_len, LANES), imap)
    o, lse = pl.pallas_call(
        functools.partial(_dil_kernel, dilation=dilation, head0=group * DIL_HEADS_PER_GROUP),
        grid=(batch, tiles_per_group, dilation),
        in_specs=[pl.BlockSpec(memory_space=pltpu.SMEM),
                  blk_spec(in_map(0)), blk_spec(in_map(1)), blk_spec(in_map(2))],
        out_specs=[blk_spec(out_map), blk_spec(out_map)],
        out_shape=[jax.ShapeDtypeStruct((batch, sub_len, dilation * DIL_OUT), BF16),
                   jax.ShapeDtypeStruct((batch, sub_len, dilation * DIL_OUT), F32)],
        scratch_shapes=[pltpu.VMEM((HEADS_PER_TILE, sub_len, LANES), BF16),
                        pltpu.VMEM((HEADS_PER_TILE, DIL_STEPS, 2 * DIL_STEPS), F32)],
        compiler_params=pltpu.CompilerParams(
            dimension_semantics=("parallel", "parallel", "parallel"), vmem_limit_bytes=VMEM_LIMIT),
        name=f"dilated_attn_g{group}",
    )(slopes, view, view, view)
    return o.reshape(batch * seq, DIL_OUT), lse.reshape(batch * seq, DIL_OUT)


def _moba_kernel(shi_ref, slo_ref, q_ref, k_ref, v_ref, o_ref, qa_ref, ka_ref, o0_ref):
    jp = pl.program_id(1)
    seq = q_ref.shape[1]
    nb = seq // MOBA_BLOCK
    shift = MOBA_BLOCK.bit_length() - 1

    q2 = q_ref[0]
    k2 = k_ref[0]
    lane = lax.broadcasted_iota(jnp.int32, (seq, LANES), 1)
    row = lax.broadcasted_iota(jnp.int32, (seq, LANES), 0)
    own = row >> shift
    jj = (row & (MOBA_BLOCK - 1)).astype(F32)
    a_row = lax.broadcasted_iota(jnp.int32, (LANES, seq), 0)
    a_col = lax.broadcasted_iota(jnp.int32, (LANES, seq), 1)
    km_lane = lax.broadcasted_iota(jnp.int32, (LANES, LANES), 1)

    for hh in range(HEADS_PER_TILE):
        head = DIL_HEADS + HEADS_PER_TILE * jp + hh
        extra0 = HEAD_DIM * (1 - hh)
        in_half = (lane >= hh * HEAD_DIM) & (lane < (hh + 1) * HEAD_DIM)
        e = lane - extra0

        avg = jnp.where((a_row - extra0) == (a_col >> shift), 1.0 / MOBA_BLOCK, 0.0).astype(BF16)
        km = jnp.dot(avg, k2, preferred_element_type=F32)
        km = jnp.where((km_lane >= hh * HEAD_DIM) & (km_lane < (hh + 1) * HEAD_DIM), km, 0.0)
        km_hi = km.astype(BF16)
        rem = km - km_hi.astype(F32)
        km_mid = rem.astype(BF16)
        km_lo = (rem - km_mid.astype(F32)).astype(BF16)
        gate = (lax.dot_general(q2, km_hi, _NT, preferred_element_type=F32)
                + lax.dot_general(q2, km_mid, _NT, preferred_element_type=F32)
                + lax.dot_general(q2, km_lo, _NT, preferred_element_type=F32))

        is_past = (e >= 0) & (e < own)
        g = jnp.where(is_past, gate, -jnp.inf)
        cnt = jnp.zeros((seq, LANES), F32)
        for m in range(nb):
            col = g[:, extra0 + m:extra0 + m + 1]
            tie = jnp.where(e > m, 1.0, 0.0)
            cnt = cnt + jnp.where(col > g, 1.0, jnp.where(col == g, tie, 0.0))
        keep = (is_past & (cnt < MOBA_TOPK)) | (e == own)
        q_extra = jnp.where(keep, 0.0, MASKED)
        q_extra = jnp.where(e == nb, shi_ref[head], jnp.where(e == nb + 1, slo_ref[head], q_extra))
        q_extra = jnp.where((e >= 0) & (e < nb + 2), q_extra, 0.0)
        qa_ref[hh] = jnp.where(in_half, q2 * jnp.asarray(SCALE, BF16), q_extra.astype(BF16))

        k_extra = jnp.where(e == own, 1.0, 0.0)
        k_extra = jnp.where((e == nb) | (e == nb + 1), jj, k_extra)
        k_extra = jnp.where((e >= 0) & (e < nb + 2), k_extra, 0.0)
        ka_ref[hh] = jnp.where(in_half, k2, k_extra.astype(BF16))

    ti = lax.broadcasted_iota(jnp.int32, (MOBA_BLOCK, MOBA_BLOCK), 0)
    tj = lax.broadcasted_iota(jnp.int32, (MOBA_BLOCK, MOBA_BLOCK), 1)
    causal = jnp.where(ti >= tj, 0.0, MASKED)
    first_half = lax.broadcasted_iota(jnp.int32, (MOBA_BLOCK, LANES), 1) < HEAD_DIM

    for hh in range(HEADS_PER_TILE):
        head = DIL_HEADS + HEADS_PER_TILE * jp + hh
        block_slope = (shi_ref[head] + slo_ref[head]) * float(MOBA_BLOCK)

        def chunk(c, carry, hh=hh, block_slope=block_slope):
            rows = pl.ds(pl.multiple_of(c * MOBA_BLOCK, MOBA_BLOCK), MOBA_BLOCK)
            qa = qa_ref[hh, rows, :]
            s = lax.dot_general(qa, ka_ref[hh, rows, :], _NT, preferred_element_type=F32) + causal
            m = jnp.max(s, axis=-1, keepdims=True)
            p = jnp.exp(s - m)
            l = jnp.sum(p, axis=-1, keepdims=True)
            acc = jnp.dot(p.astype(BF16), v_ref[0, rows, :], preferred_element_type=F32)

            def past(n, state):
                m, l, acc = state
                krows = pl.ds(pl.multiple_of(n * MOBA_BLOCK, MOBA_BLOCK), MOBA_BLOCK)
                s = lax.dot_general(qa, ka_ref[hh, krows, :], _NT, preferred_element_type=F32)
                shift_n = block_slope * (n - c).astype(F32)
                m_new = jnp.maximum(m, jnp.max(s, axis=-1, keepdims=True) + shift_n)
                alpha = jnp.exp(m - m_new)
                p = jnp.exp(s - (m_new - shift_n))
                l = alpha * l + jnp.sum(p, axis=-1, keepdims=True)
                acc = alpha * acc + jnp.dot(p.astype(BF16), v_ref[0, krows, :],
                                            preferred_element_type=F32)
                return m_new, l, acc

            m, l, acc = lax.fori_loop(0, c, past, (m, l, acc))
            out = acc / l
            if hh == 0:
                o0_ref[rows, :] = out
            else:
                o_ref[0, rows, :] = jnp.where(first_half, o0_ref[rows, :], out).astype(o_ref.dtype)
            return carry

        lax.fori_loop(0, nb, chunk, 0)


def _moba_attention(slopes_hi, slopes_lo, qkv, batch, seq):
    tiles_per_part = MOBA_WIDTH // LANES
    view = qkv.reshape(batch, seq, qkv.shape[-1])
    spec = lambda part: pl.BlockSpec((1, seq, LANES), lambda b, jp: (b, 0, part * tiles_per_part + jp))
    smem = pl.BlockSpec(memory_space=pltpu.SMEM)
    out = pl.pallas_call(
        _moba_kernel,
        grid=(batch, tiles_per_part),
        in_specs=[smem, smem, spec(0), spec(1), spec(2)],
        out_specs=pl.BlockSpec((1, seq, LANES), lambda b, jp: (b, 0, jp)),
        out_shape=jax.ShapeDtypeStruct((batch, seq, MOBA_WIDTH), BF16),
        scratch_shapes=[pltpu.VMEM((HEADS_PER_TILE, seq, LANES), BF16),
                        pltpu.VMEM((HEADS_PER_TILE, seq, LANES), BF16),
                        pltpu.VMEM((seq, LANES), F32)],
        compiler_params=pltpu.CompilerParams(dimension_semantics=("parallel", "parallel"),
                                             vmem_limit_bytes=VMEM_LIMIT),
        name="moba_attn",
    )(slopes_hi, slopes_lo, view, view, view)
    return out.reshape(batch * seq, MOBA_WIDTH)


def _mix_out_kernel(x_ref, o0_ref, o1_ref, o2_ref, l0_ref, l1_ref, l2_ref, ob_ref, ga_ref, gb_ref,
                    wa_ref, wb_ref, wo_ref, g_ref, out_ref):
    l0, l1, l2 = l0_ref[...], l1_ref[...], l2_ref[...]
    top = jnp.maximum(jnp.maximum(l0, l1), l2)
    e0, e1, e2 = jnp.exp(l0 - top), jnp.exp(l1 - top), jnp.exp(l2 - top)
    inv = 1.0 / (e0 + e1 + e2)
    o_a = (e0 * inv) * o0_ref[...].astype(F32) + (e1 * inv) * o1_ref[...].astype(F32) \
        + (e2 * inv) * o2_ref[...].astype(F32)
    y_a = jnp.dot(o_a.astype(BF16), wa_ref[...], preferred_element_type=F32)
    y_b = jnp.dot(ob_ref[...], wb_ref[...], preferred_element_type=F32)
    sig_a = 1.0 / (1.0 + jnp.exp(-ga_ref[...].astype(F32)))
    sig_b = 1.0 / (1.0 + jnp.exp(-gb_ref[...].astype(F32)))
    merged = sig_a * y_a + sig_b * y_b
    z = jnp.dot(merged.astype(BF16), wo_ref[...], preferred_element_type=F32)
    out_ref[...] = x_ref[...] + _rms_norm(z, g_ref[...])


def _mix_out(x, dil_outs, dil_lses, o_moba, gates, w_a, w_b, w_o, g_post, tm):
    n, d = x.shape
    row = lambda w, col=0: pl.BlockSpec((tm, w), lambda i: (i, col))
    return pl.pallas_call(
        _mix_out_kernel,
        grid=(n // tm,),
        in_specs=[row(d)] + [row(DIL_OUT)] * 6 + [row(MOBA_WIDTH), row(d, 0), row(d, 1),
                  _resident(w_a.shape), _resident(w_b.shape), _resident(w_o.shape), _resident((1, d))],
        out_specs=row(d),
        out_shape=jax.ShapeDtypeStruct((n, d), F32),
        compiler_params=pltpu.CompilerParams(dimension_semantics=("parallel",),
                                             vmem_limit_bytes=VMEM_LIMIT),
        name="mix_out",
    )(x, *dil_outs, *dil_lses, o_moba, gates, gates, w_a, w_b, w_o, g_post)


CARRY_ROWS = 8


def _ffn_kernel(x_ref, gpre_ref, gpost_ref, wg_ref, wu_ref, cw_ref, cb_ref, wd_ref, out_ref, a_ref,
                *, tiles_per_seq):
    tm = x_ref.shape[0]
    x = x_ref[...]
    h = _rms_norm(x, gpre_ref[...]).astype(BF16)

    @pl.when(pl.program_id(0) % tiles_per_seq == 0)
    def _():
        a_ref[0:CARRY_ROWS, :] = jnp.zeros((CARRY_ROWS, a_ref.shape[1]), F32)

    a_ref[CARRY_ROWS:CARRY_ROWS + tm, :] = jnp.dot(h, wg_ref[...], preferred_element_type=F32)
    up = jnp.dot(h, wu_ref[...], preferred_element_type=F32)
    conv = cb_ref[...]
    for tap in range(CONV_WIDTH):
        back = CONV_WIDTH - 1 - tap
        conv = conv + a_ref[CARRY_ROWS - back:CARRY_ROWS - back + tm, :] * cw_ref[tap:tap + 1, :]
    a_ref[0:CARRY_ROWS, :] = a_ref[tm:tm + CARRY_ROWS, :]
    inner = 0.7978845608028654 * (conv + 0.044715 * (conv * conv * conv))
    u = (0.5 * conv * (1.0 + jnp.tanh(inner))) * up
    z = jnp.dot(u.astype(BF16), wd_ref[...], preferred_element_type=F32)
    out_ref[...] = x + _rms_norm(z, gpost_ref[...])


def _ffn(x, g_pre, g_post, w_gate, w_up, conv_w, conv_b, w_down, seq, tm):
    n, d = x.shape
    d_ff = w_gate.shape[1]
    row = pl.BlockSpec((tm, d), lambda i: (i, 0))
    return pl.pallas_call(
        functools.partial(_ffn_kernel, tiles_per_seq=seq // tm),
        grid=(n // tm,),
        in_specs=[row, _resident((1, d)), _resident((1, d)), _resident(w_gate.shape),
                  _resident(w_up.shape), _resident(conv_w.shape), _resident((1, d_ff)),
                  _resident(w_down.shape)],
        out_specs=row,
        out_shape=jax.ShapeDtypeStruct((n, d), F32),
        scratch_shapes=[pltpu.VMEM((tm + CARRY_ROWS, d_ff), F32)],
        compiler_params=pltpu.CompilerParams(dimension_semantics=("arbitrary",),
                                             vmem_limit_bytes=VMEM_LIMIT),
        name="conv_ffn",
    )(x, g_pre, g_post, w_gate, w_up, conv_w, conv_b, w_down)


def kernel(x, mix_norm_pre, mix_norm_post, w_in, w_branch_dil, w_branch_moba, w_out, ffn_norm_pre, ffn_norm_post, w_ffn_gate, w_ffn_up, ffn_conv_w, ffn_conv_b, w_ffn_down):
    batch, seq, d = x.shape
    depth = w_in.shape[0]
    tm = 512
    idx = jnp.arange(1, N_ATTN_HEADS + 1, dtype=F32)
    slopes = jnp.exp2(-8.0 * idx / N_ATTN_HEADS)
    slopes_hi = slopes.astype(BF16).astype(F32)
    slopes_lo = (slopes - slopes_hi).astype(BF16).astype(F32)
    qkv_dil_cols = 3 * DIL_WIDTH
    qkv_cols = qkv_dil_cols + 3 * MOBA_WIDTH

    xf = x.reshape(batch * seq, d)
    for l in range(depth):
        w = w_in[l].astype(BF16)
        qkv_dil, qkv_moba, gates = _inproj(
            xf, mix_norm_pre[l][None], w[:, :qkv_dil_cols], w[:, qkv_dil_cols:qkv_cols],
            w[:, qkv_cols:], tm)
        dil = [_dilated_attention(slopes, qkv_dil, batch, seq, g) for g in range(len(DIL_GROUPS))]
        o_moba = _moba_attention(slopes_hi, slopes_lo, qkv_moba, batch, seq)
        xf = _mix_out(xf, [o for o, _ in dil], [s for _, s in dil], o_moba, gates,
                      w_branch_dil[l].astype(BF16), w_branch_moba[l].astype(BF16),
                      w_out[l].astype(BF16), mix_norm_post[l][None], tm)
        xf = _ffn(xf, ffn_norm_pre[l][None], ffn_norm_post[l][None], w_ffn_gate[l].astype(BF16),
                  w_ffn_up[l].astype(BF16), ffn_conv_w[l], ffn_conv_b[l][None],
                  w_ffn_down[l].astype(BF16), seq, tm)
    return xf.reshape(batch, seq, d)
```

```python
import functools

import jax
import jax.numpy as jnp
from jax import lax
from jax.experimental import pallas as pl
from jax.experimental.pallas import tpu as pltpu

F32 = jnp.float32
BF16 = jnp.bfloat16

HEAD_DIM = 64
DIL_GROUPS = ((128, 1), (512, 4), (2048, 16))
DIL_HEADS_PER_GROUP = 4
DIL_HEADS = DIL_HEADS_PER_GROUP * len(DIL_GROUPS)
MOBA_HEADS = 4
MOBA_BLOCK = 256
MOBA_TOPK = 3
N_ATTN_HEADS = DIL_HEADS + MOBA_HEADS
DIL_WIDTH = DIL_HEADS * HEAD_DIM
MOBA_WIDTH = MOBA_HEADS * HEAD_DIM
DIL_OUT = DIL_HEADS_PER_GROUP * HEAD_DIM
DIL_STEPS = 128
RMS_EPS = 1e-6
SCALE = HEAD_DIM ** -0.5
CONV_WIDTH = 3

LANES = 128
HEADS_PER_TILE = LANES // HEAD_DIM
MASKED = -(2.0 ** 100)
VMEM_LIMIT = 56 * 1024 * 1024
PERM_ROWS = 256

_NT = (((1,), (1,)), ((), ()))


def _rms_norm(x, g):
    return x * lax.rsqrt(jnp.mean(x * x, axis=-1, keepdims=True) + RMS_EPS) * g


def _resident(shape):
    return pl.BlockSpec(shape, lambda *_: (0,) * len(shape), pipeline_mode=pl.Buffered(1))


def _split3(x):
    hi = x.astype(BF16)
    rem = x - hi.astype(F32)
    mid = rem.astype(BF16)
    lo = (rem - mid.astype(F32)).astype(BF16)
    return hi, mid, lo


def _inproj_kernel(x_ref, g_ref, wd0_ref, wd1_ref, wd2_ref, wm_ref, wg_ref,
                   d0_ref, d1_ref, d2_ref, moba_ref, gate_ref):
    tm = x_ref.shape[0]
    h = _rms_norm(x_ref[...], g_ref[...]).astype(BF16)
    d0_ref[0, 0] = jnp.dot(h, wd0_ref[...], preferred_element_type=F32).astype(BF16)
    for w_ref, o_ref, (_, dil) in ((wd1_ref, d1_ref, DIL_GROUPS[1]), (wd2_ref, d2_ref, DIL_GROUPS[2])):
        res = jnp.dot(h, w_ref[...], preferred_element_type=F32).astype(BF16)
        per = PERM_ROWS // dil
        dst = lax.broadcasted_iota(jnp.int32, (PERM_ROWS, PERM_ROWS), 0)
        src = lax.broadcasted_iota(jnp.int32, (PERM_ROWS, PERM_ROWS), 1)
        perm = jnp.where(src == (dst % per) * dil + dst // per, 1.0, 0.0).astype(BF16)
        for s in range(tm // PERM_ROWS):
            slab = jnp.dot(perm, res[s * PERM_ROWS:(s + 1) * PERM_ROWS, :],
                           preferred_element_type=F32).astype(BF16)
            for r in range(dil):
                o_ref[0, r, s * per:(s + 1) * per, :] = slab[r * per:(r + 1) * per, :]
    moba_ref[...] = jnp.dot(h, wm_ref[...], preferred_element_type=F32).astype(BF16)
    chunk = 512
    for c in range(0, wg_ref.shape[1], chunk):
        gate_ref[:, c:c + chunk] = jnp.dot(
            h, wg_ref[:, c:c + chunk], preferred_element_type=F32).astype(BF16)


def _inproj(x, g, w_groups, w_moba, w_gate, batch, seq, tm):
    n, d = x.shape
    tiles_per_seq = seq // tm
    group_w = w_groups[0].shape[1]

    def group_spec(dil):
        return pl.BlockSpec((1, dil, tm // dil, group_w),
                            lambda i: (i // tiles_per_seq, 0, i % tiles_per_seq, 0))

    dils = [dil for _, dil in DIL_GROUPS]
    return pl.pallas_call(
        _inproj_kernel,
        grid=(n // tm,),
        in_specs=[pl.BlockSpec((tm, d), lambda i: (i, 0)), _resident((1, d))]
                 + [_resident(w.shape) for w in w_groups]
                 + [_resident(w_moba.shape), _resident(w_gate.shape)],
        out_specs=[group_spec(dil) for dil in dils]
                  + [pl.BlockSpec((tm, w_moba.shape[1]), lambda i: (i, 0)),
                     pl.BlockSpec((tm, w_gate.shape[1]), lambda i: (i, 0))],
        out_shape=[jax.ShapeDtypeStruct((batch, dil, seq // dil, group_w), BF16) for dil in dils]
                  + [jax.ShapeDtypeStruct((n, w_moba.shape[1]), BF16),
                     jax.ShapeDtypeStruct((n, w_gate.shape[1]), BF16)],
        compiler_params=pltpu.CompilerParams(dimension_semantics=("parallel",),
                                             vmem_limit_bytes=VMEM_LIMIT),
        name="inproj",
    )(x, g, *w_groups, w_moba, w_gate)


def _dil_kernel(slopes_ref, q_ref, k_ref, v_ref, o_ref, lse_ref,
                qm_ref, vt_ref, bias_ref, ot_ref, lt_ref, *, dilation, head0):
    jp = pl.program_id(1)
    n_res, sub_len = q_ref.shape[1], q_ref.shape[2]
    blk = DIL_STEPS
    nblk = sub_len // blk

    kk = lax.broadcasted_iota(jnp.int32, (2 * blk, 2 * blk), 0)
    col = lax.broadcasted_iota(jnp.int32, (2 * blk, 2 * blk), 1)
    delta = blk + (col & (blk - 1)) - kk
    slope = jnp.where(col < blk, slopes_ref[head0 + HEADS_PER_TILE * jp],
                      slopes_ref[head0 + HEADS_PER_TILE * jp + 1]) * float(dilation)
    bias_ref[...] = jnp.where((delta >= 0) & (delta <= DIL_STEPS), -slope * delta.astype(F32), MASKED)

    lane = lax.broadcasted_iota(jnp.int32, (sub_len, LANES), 1)
    for r in range(n_res):
        q2 = q_ref[0, r]
        for hh in range(HEADS_PER_TILE):
            in_half = (lane >= hh * HEAD_DIM) & (lane < (hh + 1) * HEAD_DIM)
            qm_ref[hh, r] = jnp.where(in_half, q2, jnp.zeros_like(q2)) * jnp.asarray(SCALE, BF16)
        vt_ref[r] = v_ref[0, r].astype(F32).T.astype(BF16)

    for r in range(n_res):
        for n in range(nblk):
            q_rows = slice(n * blk, (n + 1) * blk)
            k_rows = slice((n - 1) * blk, (n + 1) * blk) if n > 0 else q_rows
            queries = jnp.concatenate([qm_ref[0, r, q_rows, :], qm_ref[1, r, q_rows, :]], axis=0)
            s = lax.dot_general(k_ref[0, r, k_rows, :], queries, _NT, preferred_element_type=F32)
            s = s + (bias_ref[...] if n > 0 else bias_ref[blk:, :])
            m = jnp.max(s, axis=0, keepdims=True)
            p = jnp.exp(s - m)
            l = jnp.sum(p, axis=0, keepdims=True)
            o = jnp.dot(vt_ref[r, :, k_rows], p.astype(BF16), preferred_element_type=F32)
            inv = 1.0 / l
            lse = m + jnp.log(l)
            for hh in range(HEADS_PER_TILE):
                feat = slice(hh * HEAD_DIM, (hh + 1) * HEAD_DIM)
                qcol = slice(hh * blk, (hh + 1) * blk)
                ot_ref[r, feat, q_rows] = o[feat, qcol] * inv[:, qcol]
                lt_ref[r, feat, q_rows] = jnp.broadcast_to(lse[:, qcol], (HEAD_DIM, blk))

    for r in range(n_res):
        o_ref[0, r] = ot_ref[r].T.astype(o_ref.dtype)
        lse_ref[0, r] = lt_ref[r].T


def _dilated_attention(slopes, qkv, group):
    _, dilation = DIL_GROUPS[group]
    batch, n_res, sub_len, _ = qkv.shape
    tiles_per_part = DIL_OUT // LANES

    def spec(part):
        return pl.BlockSpec((1, n_res, sub_len, LANES), lambda b, jp: (b, 0, 0, part * tiles_per_part + jp))

    return pl.pallas_call(
        functools.partial(_dil_kernel, dilation=dilation, head0=group * DIL_HEADS_PER_GROUP),
        grid=(batch, tiles_per_part),
        in_specs=[pl.BlockSpec(memory_space=pltpu.SMEM), spec(0), spec(1), spec(2)],
        out_specs=[spec(0), spec(0)],
        out_shape=[jax.ShapeDtypeStruct((batch, n_res, sub_len, DIL_OUT), BF16),
                   jax.ShapeDtypeStruct((batch, n_res, sub_len, DIL_OUT), F32)],
        scratch_shapes=[pltpu.VMEM((HEADS_PER_TILE, n_res, sub_len, LANES), BF16),
                        pltpu.VMEM((n_res, LANES, sub_len), BF16),
                        pltpu.VMEM((2 * DIL_STEPS, 2 * DIL_STEPS), F32),
                        pltpu.VMEM((n_res, LANES, sub_len), F32),
                        pltpu.VMEM((n_res, LANES, sub_len), F32)],
        compiler_params=pltpu.CompilerParams(dimension_semantics=("parallel", "parallel"),
                                             vmem_limit_bytes=VMEM_LIMIT),
        name=f"dilated_attn_g{group}",
    )(slopes, qkv, qkv, qkv)


N_MOBA_BLOCKS = 8


def _moba_kernel(slopes_ref, q_ref, k_ref, v_ref, o_ref, qa_ref, ka_ref, vt_ref, ot_ref):
    jp = pl.program_id(1)
    seq = q_ref.shape[1]
    nb = N_MOBA_BLOCKS
    shift = MOBA_BLOCK.bit_length() - 1

    q2 = q_ref[0]
    k2 = k_ref[0]
    vt_ref[...] = v_ref[0].astype(F32).T.astype(BF16)

    lane = lax.broadcasted_iota(jnp.int32, (seq, LANES), 1)
    row = lax.broadcasted_iota(jnp.int32, (seq, LANES), 0)
    e_any = lane & (HEAD_DIM - 1)
    k_extra = jnp.where(
        e_any < 4 * nb, jnp.where((e_any & (nb - 1)) == (row >> shift), 1.0, 0.0),
        jnp.where(e_any < 4 * nb + 3, (row & (MOBA_BLOCK - 1)).astype(F32), 0.0)).astype(BF16)

    lane1 = lax.broadcasted_iota(jnp.int32, (1, LANES), 1)
    e1 = lane1 & (HEAD_DIM - 1)
    a_row = lax.broadcasted_iota(jnp.int32, (LANES, seq), 0)
    a_col = lax.broadcasted_iota(jnp.int32, (LANES, seq), 1)
    km_lane = lax.broadcasted_iota(jnp.int32, (LANES, LANES), 1)
    blk_idx = lax.broadcasted_iota(jnp.int32, (nb, seq), 0)
    own_blk = lax.broadcasted_iota(jnp.int32, (nb, seq), 1) >> shift

    for hh in range(HEADS_PER_TILE):
        slope = slopes_ref[DIL_HEADS + HEADS_PER_TILE * jp + hh]
        extra0 = HEAD_DIM * (1 - hh)
        in_half = (lane >= hh * HEAD_DIM) & (lane < (hh + 1) * HEAD_DIM)

        avg = jnp.where((a_row - extra0) == (a_col >> shift), 1.0 / MOBA_BLOCK, 0.0).astype(BF16)
        km = jnp.dot(avg, k2, preferred_element_type=F32)
        km = jnp.where((km_lane >= hh * HEAD_DIM) & (km_lane < (hh + 1) * HEAD_DIM), km, 0.0)
        gate_t = sum(lax.dot_general(part, q2, _NT, preferred_element_type=F32) for part in _split3(km))
        g = gate_t[extra0:extra0 + nb, :]

        is_past = blk_idx < own_blk
        g = jnp.where(is_past, g, -jnp.inf)
        cnt = jnp.zeros((nb, seq), F32)
        for m in range(nb):
            tie = jnp.where(blk_idx > m, 1.0, 0.0)
            cnt = cnt + jnp.where(g[m:m + 1, :] > g, 1.0, jnp.where(g[m:m + 1, :] == g, tie, 0.0))
        keep = (is_past & (cnt < MOBA_TOPK)) | (blk_idx == own_blk)
        sel = jnp.where(keep, 0.0, MASKED)
        pieces = [sel, jnp.zeros((LANES - extra0 - nb, seq), F32)]
        if extra0:
            pieces.insert(0, jnp.zeros((extra0, seq), F32))
        sel_q = jnp.concatenate(pieces, axis=0).T

        base = jnp.where(e1 < 4 * nb, slope * float(MOBA_BLOCK) * (e1 & (nb - 1)).astype(F32), slope)
        term = jnp.where(e1 < 4 * nb, (e1 >> 3) - 1, e1 - 4 * nb)
        hi, mid, lo = _split3(base)
        q_const = jnp.where(term == 0, hi, jnp.where(term == 1, mid, jnp.where(term == 2, lo, jnp.zeros_like(hi))))
        q_extra = jnp.where((lane >= extra0) & (lane < extra0 + nb), sel_q.astype(BF16), q_const)
        qa_ref[hh] = jnp.where(in_half, q2 * jnp.asarray(SCALE, BF16), q_extra)
        ka_ref[hh] = jnp.where(in_half, k2, k_extra)

    ki = lax.broadcasted_iota(jnp.int32, (MOBA_BLOCK, MOBA_BLOCK), 0)
    qi = lax.broadcasted_iota(jnp.int32, (MOBA_BLOCK, MOBA_BLOCK), 1)
    causal = jnp.where(ki <= qi, 0.0, MASKED)

    for hh in range(HEADS_PER_TILE):
        feat = slice(hh * HEAD_DIM, (hh + 1) * HEAD_DIM)
        for c in range(nb):
            start = c * MOBA_BLOCK
            stop = start + MOBA_BLOCK
            queries = qa_ref[hh, start:stop, :]
            s_own = lax.dot_general(ka_ref[hh, start:stop, :], queries, _NT,
                                    preferred_element_type=F32) + causal
            m = jnp.max(s_own, axis=0, keepdims=True)
            if c:
                s_past = lax.dot_general(ka_ref[hh, :start, :], queries, _NT, preferred_element_type=F32)
                m = jnp.maximum(m, jnp.max(s_past, axis=0, keepdims=True))
            p_own = jnp.exp(s_own - m)
            l = jnp.sum(p_own, axis=0, keepdims=True)
            o = jnp.dot(vt_ref[feat, start:stop], p_own.astype(BF16), preferred_element_type=F32)
            if c:
                p_past = jnp.exp(s_past - m)
                l = l + jnp.sum(p_past, axis=0, keepdims=True)
                o = o + jnp.dot(vt_ref[feat, :start], p_past.astype(BF16), preferred_element_type=F32)
            ot_ref[feat, start:stop] = o * (1.0 / l)

    o_ref[0] = ot_ref[...].T.astype(o_ref.dtype)


def _moba_attention(slopes, qkv, batch, seq):
    assert seq == N_MOBA_BLOCKS * MOBA_BLOCK
    tiles_per_part = MOBA_WIDTH // LANES
    view = qkv.reshape(batch, seq, qkv.shape[-1])
    spec = lambda part: pl.BlockSpec((1, seq, LANES), lambda b, jp: (b, 0, part * tiles_per_part + jp))
    out = pl.pallas_call(
        _moba_kernel,
        grid=(batch, tiles_per_part),
        in_specs=[pl.BlockSpec(memory_space=pltpu.SMEM), spec(0), spec(1), spec(2)],
        out_specs=spec(0),
        out_shape=jax.ShapeDtypeStruct((batch, seq, MOBA_WIDTH), BF16),
        scratch_shapes=[pltpu.VMEM((HEADS_PER_TILE, seq, LANES), BF16),
                        pltpu.VMEM((HEADS_PER_TILE, seq, LANES), BF16),
                        pltpu.VMEM((LANES, seq), BF16),
                        pltpu.VMEM((LANES, seq), F32)],
        compiler_params=pltpu.CompilerParams(dimension_semantics=("parallel", "parallel"),
                                             vmem_limit_bytes=VMEM_LIMIT),
        name="moba_attn",
    )(slopes, view, view, view)
    return out.reshape(batch * seq, MOBA_WIDTH)


def _mix_out_kernel(x_ref, o0_ref, o1_ref, o2_ref, l0_ref, l1_ref, l2_ref, ob_ref, ga_ref, gb_ref,
                    wa_ref, wb_ref, wo_ref, g_ref, out_ref, nat_ref):
    def natural(ref, slot):
        dil, per = ref.shape[1], ref.shape[2]
        if dil == 1:
            return ref[0, 0].astype(F32)
        for r in range(dil):
            val = ref[0, r].astype(F32)
            for t in range(val.shape[1] // LANES):
                nat_ref.at[slot, t][pl.ds(r, per, stride=dil), :] = val[:, t * LANES:(t + 1) * LANES]
        return jnp.concatenate([nat_ref[slot, t] for t in range(ref.shape[3] // LANES)], axis=1)

    lses = [natural(ref, i) for i, ref in enumerate((l0_ref, l1_ref, l2_ref))]
    outs = [natural(ref, 3 + i) for i, ref in enumerate((o0_ref, o1_ref, o2_ref))]
    top = jnp.maximum(jnp.maximum(lses[0], lses[1]), lses[2])
    es = [jnp.exp(l - top) for l in lses]
    inv = 1.0 / (es[0] + es[1] + es[2])
    o_a = (es[0] * inv) * outs[0] + (es[1] * inv) * outs[1] + (es[2] * inv) * outs[2]
    y_a = jnp.dot(o_a.astype(BF16), wa_ref[...], preferred_element_type=F32)
    y_b = jnp.dot(ob_ref[...], wb_ref[...], preferred_element_type=F32)
    sig_a = 1.0 / (1.0 + jnp.exp(-ga_ref[...].astype(F32)))
    sig_b = 1.0 / (1.0 + jnp.exp(-gb_ref[...].astype(F32)))
    merged = sig_a * y_a + sig_b * y_b
    z = jnp.dot(merged.astype(BF16), wo_ref[...], preferred_element_type=F32)
    out_ref[...] = x_ref[...] + _rms_norm(z, g_ref[...])


def _mix_out(x, dil_outs, dil_lses, o_moba, gates, w_a, w_b, w_o, g_post, seq, tm):
    n, d = x.shape
    tiles_per_seq = seq // tm
    row = lambda w, col=0: pl.BlockSpec((tm, w), lambda i: (i, col))

    def group_spec(arr):
        dil = arr.shape[1]
        return pl.BlockSpec((1, dil, tm // dil, arr.shape[3]),
                            lambda i: (i // tiles_per_seq, 0, i % tiles_per_seq, 0))

    return pl.pallas_call(
        _mix_out_kernel,
        grid=(n // tm,),
        in_specs=[row(d)] + [group_spec(a) for a in dil_outs] + [group_spec(a) for a in dil_lses]
                 + [row(MOBA_WIDTH), row(d, 0), row(d, 1),
                    _resident(w_a.shape), _resident(w_b.shape), _resident(w_o.shape), _resident((1, d))],
        out_specs=row(d),
        out_shape=jax.ShapeDtypeStruct((n, d), F32),
        scratch_shapes=[pltpu.VMEM((2 * len(DIL_GROUPS), DIL_OUT // LANES, tm, LANES), F32)],
        compiler_params=pltpu.CompilerParams(dimension_semantics=("parallel",),
                                             vmem_limit_bytes=VMEM_LIMIT),
        name="mix_out",
    )(x, *dil_outs, *dil_lses, o_moba, gates, gates, w_a, w_b, w_o, g_post)


CARRY_ROWS = 8


def _ffn_kernel(x_ref, gpre_ref, gpost_ref, wg_ref, wu_ref, cw_ref, cb_ref, wd_ref, out_ref, a_ref,
                *, tiles_per_seq):
    tm = x_ref.shape[0]
    x = x_ref[...]
    h = _rms_norm(x, gpre_ref[...]).astype(BF16)

    @pl.when(pl.program_id(0) % tiles_per_seq == 0)
    def _():
        a_ref[0:CARRY_ROWS, :] = jnp.zeros((CARRY_ROWS, a_ref.shape[1]), F32)

    a_ref[CARRY_ROWS:CARRY_ROWS + tm, :] = jnp.dot(h, wg_ref[...], preferred_element_type=F32)
    up = jnp.dot(h, wu_ref[...], preferred_element_type=F32)
    conv = cb_ref[...]
    for tap in range(CONV_WIDTH):
        back = CONV_WIDTH - 1 - tap
        conv = conv + a_ref[CARRY_ROWS - back:CARRY_ROWS - back + tm, :] * cw_ref[tap:tap + 1, :]
    a_ref[0:CARRY_ROWS, :] = a_ref[tm:tm + CARRY_ROWS, :]
    inner = 0.7978845608028654 * (conv + 0.044715 * (conv * conv * conv))
    u = (0.5 * conv * (1.0 + jnp.tanh(inner))) * up
    z = jnp.dot(u.astype(BF16), wd_ref[...], preferred_element_type=F32)
    out_ref[...] = x + _rms_norm(z, gpost_ref[...])


def _ffn(x, g_pre, g_post, w_gate, w_up, conv_w, conv_b, w_down, seq, tm):
    n, d = x.shape
    d_ff = w_gate.shape[1]
    row = pl.BlockSpec((tm, d), lambda i: (i, 0))
    return pl.pallas_call(
        functools.partial(_ffn_kernel, tiles_per_seq=seq // tm),
        grid=(n // tm,),
        in_specs=[row, _resident((1, d)), _resident((1, d)), _resident(w_gate.shape),
                  _resident(w_up.shape), _resident(conv_w.shape), _resident((1, d_ff)),
                  _resident(w_down.shape)],
        out_specs=row,
        out_shape=jax.ShapeDtypeStruct((n, d), F32),
        scratch_shapes=[pltpu.VMEM((tm + CARRY_ROWS, d_ff), F32)],
        compiler_params=pltpu.CompilerParams(dimension_semantics=("arbitrary",),
                                             vmem_limit_bytes=VMEM_LIMIT),
        name="conv_ffn",
    )(x, g_pre, g_post, w_gate, w_up, conv_w, conv_b, w_down)


def kernel(x, mix_norm_pre, mix_norm_post, w_in, w_branch_dil, w_branch_moba, w_out, ffn_norm_pre, ffn_norm_post, w_ffn_gate, w_ffn_up, ffn_conv_w, ffn_conv_b, w_ffn_down):
    batch, seq, d = x.shape
    depth = w_in.shape[0]
    tm = 512
    idx = jnp.arange(1, N_ATTN_HEADS + 1, dtype=F32)
    slopes = jnp.exp2(-8.0 * idx / N_ATTN_HEADS)
    qkv_dil_cols = 3 * DIL_WIDTH
    qkv_cols = qkv_dil_cols + 3 * MOBA_WIDTH

    xf = x.reshape(batch * seq, d)
    for l in range(depth):
        w = w_in[l].astype(BF16)
        w_groups = [jnp.concatenate([w[:, part * DIL_WIDTH + g * DIL_OUT:part * DIL_WIDTH + (g + 1) * DIL_OUT]
                                     for part in range(3)], axis=1) for g in range(len(DIL_GROUPS))]
        *qkv_groups, qkv_moba, gates = _inproj(
            xf, mix_norm_pre[l][None], w_groups, w[:, qkv_dil_cols:qkv_cols], w[:, qkv_cols:],
            batch, seq, tm)
        dil = [_dilated_attention(slopes, qkv_groups[g], g) for g in range(len(DIL_GROUPS))]
        o_moba = _moba_attention(slopes, qkv_moba, batch, seq)
        xf = _mix_out(xf, [o for o, _ in dil], [s for _, s in dil], o_moba, gates,
                      w_branch_dil[l].astype(BF16), w_branch_moba[l].astype(BF16),
                      w_out[l].astype(BF16), mix_norm_post[l][None], seq, tm)
        xf = _ffn(xf, ffn_norm_pre[l][None], ffn_norm_post[l][None], w_ffn_gate[l].astype(BF16),
                  w_ffn_up[l].astype(BF16), ffn_conv_w[l], ffn_conv_b[l][None],
                  w_ffn_down[l].astype(BF16), seq, tm)
    return xf.reshape(batch, seq, d)
```

```python
import functools

import jax
import jax.numpy as jnp
from jax import lax
from jax.experimental import pallas as pl
from jax.experimental.pallas import tpu as pltpu

F32 = jnp.float32
BF16 = jnp.bfloat16

HEAD_DIM = 64
DIL_GROUPS = ((128, 1), (512, 4), (2048, 16))
DIL_HEADS_PER_GROUP = 4
DIL_HEADS = DIL_HEADS_PER_GROUP * len(DIL_GROUPS)
MOBA_HEADS = 4
MOBA_BLOCK = 256
MOBA_TOPK = 3
N_ATTN_HEADS = DIL_HEADS + MOBA_HEADS
DIL_WIDTH = DIL_HEADS * HEAD_DIM
MOBA_WIDTH = MOBA_HEADS * HEAD_DIM
DIL_OUT = DIL_HEADS_PER_GROUP * HEAD_DIM
DIL_STEPS = 128
RMS_EPS = 1e-6
SCALE = HEAD_DIM ** -0.5
CONV_WIDTH = 3

LANES = 128
HEADS_PER_TILE = LANES // HEAD_DIM
MASKED = -(2.0 ** 100)
VMEM_LIMIT = 56 * 1024 * 1024
PERM_ROWS = 256

LOG2E = 1.4426950408889634
ONES_ROWS = 16

_NT = (((1,), (1,)), ((), ()))


def _emit_pipelined(units, stages, lag):
    done = [dict() for _ in stages]
    for t in range(len(units) + lag * (len(stages) - 1)):
        for k, stage in enumerate(stages):
            u = t - lag * k
            if 0 <= u < len(units):
                prev = done[k - 1].pop(u) if k else ()
                done[k][u] = stage(units[u], *prev)


def _rms_norm(x, g):
    return x * lax.rsqrt(jnp.mean(x * x, axis=-1, keepdims=True) + RMS_EPS) * g


def _resident(shape):
    return pl.BlockSpec(shape, lambda *_: (0,) * len(shape), pipeline_mode=pl.Buffered(1))


def _split3(x):
    hi = x.astype(BF16)
    rem = x - hi.astype(F32)
    mid = rem.astype(BF16)
    lo = (rem - mid.astype(F32)).astype(BF16)
    return hi, mid, lo


def _inproj_kernel(x_ref, g_ref, wd0_ref, wd1_ref, wd2_ref, wm_ref, wg_ref,
                   d0_ref, d1_ref, d2_ref, moba_ref, gate_ref):
    tm = x_ref.shape[0]
    h = _rms_norm(x_ref[...], g_ref[...]).astype(BF16)
    d0_ref[0, 0] = jnp.dot(h, wd0_ref[...], preferred_element_type=F32).astype(BF16)
    for w_ref, o_ref, (_, dil) in ((wd1_ref, d1_ref, DIL_GROUPS[1]), (wd2_ref, d2_ref, DIL_GROUPS[2])):
        res = jnp.dot(h, w_ref[...], preferred_element_type=F32).astype(BF16)
        per = PERM_ROWS // dil
        dst = lax.broadcasted_iota(jnp.int32, (PERM_ROWS, PERM_ROWS), 0)
        src = lax.broadcasted_iota(jnp.int32, (PERM_ROWS, PERM_ROWS), 1)
        perm = jnp.where(src == (dst % per) * dil + dst // per, 1.0, 0.0).astype(BF16)
        for s in range(tm // PERM_ROWS):
            slab = jnp.dot(perm, res[s * PERM_ROWS:(s + 1) * PERM_ROWS, :],
                           preferred_element_type=F32).astype(BF16)
            for r in range(dil):
                o_ref[0, r, s * per:(s + 1) * per, :] = slab[r * per:(r + 1) * per, :]
    moba_ref[...] = jnp.dot(h, wm_ref[...], preferred_element_type=F32).astype(BF16)
    chunk = 512
    for c in range(0, wg_ref.shape[1], chunk):
        gate_ref[:, c:c + chunk] = jnp.dot(
            h, wg_ref[:, c:c + chunk], preferred_element_type=F32).astype(BF16)


def _inproj(x, g, w_groups, w_moba, w_gate, batch, seq, tm):
    n, d = x.shape
    tiles_per_seq = seq // tm
    group_w = w_groups[0].shape[1]

    def group_spec(dil):
        return pl.BlockSpec((1, dil, tm // dil, group_w),
                            lambda i: (i // tiles_per_seq, 0, i % tiles_per_seq, 0))

    dils = [dil for _, dil in DIL_GROUPS]
    return pl.pallas_call(
        _inproj_kernel,
        grid=(n // tm,),
        in_specs=[pl.BlockSpec((tm, d), lambda i: (i, 0)), _resident((1, d))]
                 + [_resident(w.shape) for w in w_groups]
                 + [_resident(w_moba.shape), _resident(w_gate.shape)],
        out_specs=[group_spec(dil) for dil in dils]
                  + [pl.BlockSpec((tm, w_moba.shape[1]), lambda i: (i, 0)),
                     pl.BlockSpec((tm, w_gate.shape[1]), lambda i: (i, 0))],
        out_shape=[jax.ShapeDtypeStruct((batch, dil, seq // dil, group_w), BF16) for dil in dils]
                  + [jax.ShapeDtypeStruct((n, w_moba.shape[1]), BF16),
                     jax.ShapeDtypeStruct((n, w_gate.shape[1]), BF16)],
        compiler_params=pltpu.CompilerParams(dimension_semantics=("parallel",),
                                             vmem_limit_bytes=VMEM_LIMIT),
        name="inproj",
    )(x, g, *w_groups, w_moba, w_gate)


def _dil_kernel(slopes_ref, q_ref, k_ref, v_ref, o_ref, lse_ref,
                qm_ref, vt_ref, bias_ref, ot_ref, lt_ref, *, dilation, head0):
    jp = pl.program_id(1)
    n_res, sub_len = q_ref.shape[1], q_ref.shape[2]
    blk = DIL_STEPS
    nblk = sub_len // blk

    kk = lax.broadcasted_iota(jnp.int32, (2 * blk, 2 * blk), 0)
    col = lax.broadcasted_iota(jnp.int32, (2 * blk, 2 * blk), 1)
    delta = blk + (col & (blk - 1)) - kk
    slope = jnp.where(col < blk, slopes_ref[head0 + HEADS_PER_TILE * jp],
                      slopes_ref[head0 + HEADS_PER_TILE * jp + 1]) * (float(dilation) * LOG2E)
    bias_ref[...] = jnp.where((delta >= 0) & (delta <= DIL_STEPS), -slope * delta.astype(F32), MASKED)

    lane = lax.broadcasted_iota(jnp.int32, (sub_len, LANES), 1)
    for r in range(n_res):
        q2 = q_ref[0, r]
        for hh in range(HEADS_PER_TILE):
            in_half = (lane >= hh * HEAD_DIM) & (lane < (hh + 1) * HEAD_DIM)
            qm_ref[hh, r] = jnp.where(in_half, q2, jnp.zeros_like(q2))
        vt_ref[r, :LANES, :] = v_ref[0, r].astype(F32).T.astype(BF16)
        vt_ref[r, LANES:, :] = jnp.ones((ONES_ROWS, sub_len), BF16)

    def rows_of(unit):
        r, n = unit
        q_rows = slice(n * blk, (n + 1) * blk)
        return r, q_rows, (slice((n - 1) * blk, (n + 1) * blk) if n else q_rows)

    def scores(unit):
        r, q_rows, k_rows = rows_of(unit)
        queries = jnp.concatenate([qm_ref[0, r, q_rows, :], qm_ref[1, r, q_rows, :]], axis=0)
        s = lax.dot_general(k_ref[0, r, k_rows, :], queries, _NT, preferred_element_type=F32)
        s = s + (bias_ref[...] if unit[1] else bias_ref[blk:, :])
        return s, jnp.max(s, axis=0, keepdims=True)

    def probs(unit, s, m):
        return jnp.exp2(s - m).astype(BF16), m

    def outputs(unit, p, m):
        r, q_rows, k_rows = rows_of(unit)
        o = jnp.dot(vt_ref[r, :, k_rows], p, preferred_element_type=F32)
        l = o[LANES:LANES + 1, :]
        inv = 1.0 / l
        lse = m + jnp.log2(l)
        for hh in range(HEADS_PER_TILE):
            feat = slice(hh * HEAD_DIM, (hh + 1) * HEAD_DIM)
            qcol = slice(hh * blk, (hh + 1) * blk)
            ot_ref[r, feat, q_rows] = o[feat, qcol] * inv[:, qcol]
            lt_ref[r, feat, q_rows] = jnp.broadcast_to(lse[:, qcol], (HEAD_DIM, blk))

    _emit_pipelined([(r, n) for r in range(n_res) for n in range(nblk)], (scores, probs, outputs), lag=2)

    for r in range(n_res):
        o_ref[0, r] = ot_ref[r].T.astype(o_ref.dtype)
        lse_ref[0, r] = lt_ref[r].T


def _dilated_attention(slopes, qkv, group):
    _, dilation = DIL_GROUPS[group]
    batch, n_res, sub_len, _ = qkv.shape
    tiles_per_part = DIL_OUT // LANES

    def spec(part):
        return pl.BlockSpec((1, n_res, sub_len, LANES), lambda b, jp: (b, 0, 0, part * tiles_per_part + jp))

    return pl.pallas_call(
        functools.partial(_dil_kernel, dilation=dilation, head0=group * DIL_HEADS_PER_GROUP),
        grid=(batch, tiles_per_part),
        in_specs=[pl.BlockSpec(memory_space=pltpu.SMEM), spec(0), spec(1), spec(2)],
        out_specs=[spec(0), spec(0)],
        out_shape=[jax.ShapeDtypeStruct((batch, n_res, sub_len, DIL_OUT), BF16),
                   jax.ShapeDtypeStruct((batch, n_res, sub_len, DIL_OUT), F32)],
        scratch_shapes=[pltpu.VMEM((HEADS_PER_TILE, n_res, sub_len, LANES), BF16),
                        pltpu.VMEM((n_res, LANES + ONES_ROWS, sub_len), BF16),
                        pltpu.VMEM((2 * DIL_STEPS, 2 * DIL_STEPS), F32),
                        pltpu.VMEM((n_res, LANES, sub_len), F32),
                        pltpu.VMEM((n_res, LANES, sub_len), F32)],
        compiler_params=pltpu.CompilerParams(dimension_semantics=("parallel", "parallel"),
                                             vmem_limit_bytes=VMEM_LIMIT),
        name=f"dilated_attn_g{group}",
    )(slopes, qkv, qkv, qkv)


N_MOBA_BLOCKS = 8


def _moba_kernel(slopes_ref, q_ref, k_ref, v_ref, o_ref, qa_ref, ka_ref, vt_ref, ot_ref):
    jp = pl.program_id(1)
    seq = q_ref.shape[1]
    nb = N_MOBA_BLOCKS
    shift = MOBA_BLOCK.bit_length() - 1
    sel_lanes = HEADS_PER_TILE * nb

    q2 = q_ref[0]
    k2 = k_ref[0]
    v_t = v_ref[0].astype(F32).T.astype(BF16)
    for hh in range(HEADS_PER_TILE):
        vt_ref[hh, :HEAD_DIM, :] = v_t[hh * HEAD_DIM:(hh + 1) * HEAD_DIM, :]
        vt_ref[hh, HEAD_DIM:, :] = jnp.ones((ONES_ROWS, seq), BF16)

    lane = lax.broadcasted_iota(jnp.int32, (seq, LANES), 1)
    row = lax.broadcasted_iota(jnp.int32, (seq, LANES), 0)
    one_hot = jnp.where((lane & (nb - 1)) == (row >> shift), 1.0, 0.0)
    in_block = (row & (MOBA_BLOCK - 1)).astype(F32)
    n_one_hot = sel_lanes + 3 * nb
    k_extra = jnp.where(lane < n_one_hot, one_hot, jnp.where(lane < n_one_hot + 3, in_block, 0.0))
    ka_ref[:, :LANES] = k2
    ka_ref[:, LANES:] = k_extra.astype(BF16)

    a_row = lax.broadcasted_iota(jnp.int32, (LANES, seq), 0)
    a_col = lax.broadcasted_iota(jnp.int32, (LANES, seq), 1)
    avg = jnp.where((a_row < sel_lanes) & ((a_row & (nb - 1)) == (a_col >> shift)),
                    1.0 / MOBA_BLOCK, 0.0).astype(BF16)
    km = jnp.dot(avg, k2, preferred_element_type=F32)
    km_row = lax.broadcasted_iota(jnp.int32, (LANES, LANES), 0)
    km_lane = lax.broadcasted_iota(jnp.int32, (LANES, LANES), 1)
    km = jnp.where((km_row < sel_lanes) & ((km_row >> 3) == (km_lane >> 6)), km, 0.0)
    gate_t = sum(lax.dot_general(part, q2, _NT, preferred_element_type=F32) for part in _split3(km))

    blk_idx = lax.broadcasted_iota(jnp.int32, (nb, seq), 0)
    own_blk = lax.broadcasted_iota(jnp.int32, (nb, seq), 1) >> shift
    is_past = blk_idx < own_blk
    sels = []
    for hh in range(HEADS_PER_TILE):
        g = jnp.where(is_past, gate_t[hh * nb:(hh + 1) * nb, :], -jnp.inf)
        cnt = jnp.zeros((nb, seq), F32)
        for m in range(nb):
            tie = jnp.where(blk_idx > m, 1.0, 0.0)
            cnt = cnt + jnp.where(g[m:m + 1, :] > g, 1.0, jnp.where(g[m:m + 1, :] == g, tie, 0.0))
        keep = (is_past & (cnt < MOBA_TOPK)) | (blk_idx == own_blk)
        sels.append(jnp.where(keep, 0.0, MASKED))
    sel_q = jnp.concatenate(sels + [jnp.zeros((LANES - sel_lanes, seq), F32)], axis=0).T

    lane1 = lax.broadcasted_iota(jnp.int32, (1, LANES), 1)
    for hh in range(HEADS_PER_TILE):
        slope = slopes_ref[DIL_HEADS + HEADS_PER_TILE * jp + hh] * LOG2E
        base = jnp.where(lane1 < n_one_hot, slope * float(MOBA_BLOCK) * (lane1 & (nb - 1)).astype(F32), slope)
        term = jnp.where(lane1 < n_one_hot, ((lane1 - sel_lanes) >> 3), lane1 - n_one_hot)
        term = jnp.where(lane1 < sel_lanes, -1, term)
        hi, mid, lo = _split3(base)
        q_const = jnp.where(term == 0, hi, jnp.where(term == 1, mid, jnp.where(term == 2, lo, jnp.zeros_like(hi))))
        own_sel = (lane >= hh * nb) & (lane < (hh + 1) * nb)
        q_extra = jnp.where(own_sel, sel_q.astype(BF16), q_const)
        in_half = (lane >= hh * HEAD_DIM) & (lane < (hh + 1) * HEAD_DIM)
        qa_ref[hh, :, :LANES] = jnp.where(in_half, q2, jnp.zeros_like(q2))
        qa_ref[hh, :, LANES:] = q_extra

    ki = lax.broadcasted_iota(jnp.int32, (MOBA_BLOCK, HEADS_PER_TILE * MOBA_BLOCK), 0)
    qi = lax.broadcasted_iota(jnp.int32, (MOBA_BLOCK, HEADS_PER_TILE * MOBA_BLOCK), 1) & (MOBA_BLOCK - 1)
    causal = jnp.where(ki <= qi, 0.0, MASKED)

    def scores(c):
        start, stop = c * MOBA_BLOCK, (c + 1) * MOBA_BLOCK
        queries = jnp.concatenate([qa_ref[0, start:stop, :], qa_ref[1, start:stop, :]], axis=0)
        s = lax.dot_general(ka_ref[:stop, :], queries, _NT, preferred_element_type=F32)
        s_own = s[start:, :] + causal
        m = jnp.max(s_own, axis=0, keepdims=True)
        if c:
            m = jnp.maximum(m, jnp.max(s[:start, :], axis=0, keepdims=True))
        return (s[:start, :] if c else None), s_own, m

    def probs(c, s_past, s_own, m):
        p_own = jnp.exp2(s_own - m).astype(BF16)
        if not c:
            return (p_own,)
        return (jnp.concatenate([jnp.exp2(s_past - m).astype(BF16), p_own], axis=0),)

    def outputs(c, p):
        start, stop = c * MOBA_BLOCK, (c + 1) * MOBA_BLOCK
        for hh in range(HEADS_PER_TILE):
            o = jnp.dot(vt_ref[hh, :, :stop], p[:, hh * MOBA_BLOCK:(hh + 1) * MOBA_BLOCK],
                        preferred_element_type=F32)
            ot_ref[hh * HEAD_DIM:(hh + 1) * HEAD_DIM, start:stop] = o[:HEAD_DIM] * (1.0 / o[HEAD_DIM:HEAD_DIM + 1])

    _emit_pipelined(list(range(nb)), (scores, probs, outputs), lag=1)
    o_ref[0] = ot_ref[...].T.astype(o_ref.dtype)


def _moba_attention(slopes, qkv, batch, seq):
    assert seq == N_MOBA_BLOCKS * MOBA_BLOCK
    tiles_per_part = MOBA_WIDTH // LANES
    view = qkv.reshape(batch, seq, qkv.shape[-1])
    spec = lambda part: pl.BlockSpec((1, seq, LANES), lambda b, jp: (b, 0, part * tiles_per_part + jp))
    out = pl.pallas_call(
        _moba_kernel,
        grid=(batch, tiles_per_part),
        in_specs=[pl.BlockSpec(memory_space=pltpu.SMEM), spec(0), spec(1), spec(2)],
        out_specs=spec(0),
        out_shape=jax.ShapeDtypeStruct((batch, seq, MOBA_WIDTH), BF16),
        scratch_shapes=[pltpu.VMEM((HEADS_PER_TILE, seq, 2 * LANES), BF16),
                        pltpu.VMEM((seq, 2 * LANES), BF16),
                        pltpu.VMEM((HEADS_PER_TILE, HEAD_DIM + ONES_ROWS, seq), BF16),
                        pltpu.VMEM((LANES, seq), F32)],
        compiler_params=pltpu.CompilerParams(dimension_semantics=("parallel", "parallel"),
                                             vmem_limit_bytes=VMEM_LIMIT),
        name="moba_attn",
    )(slopes, view, view, view)
    return out.reshape(batch * seq, MOBA_WIDTH)


def _mix_out_kernel(x_ref, o0_ref, o1_ref, o2_ref, l0_ref, l1_ref, l2_ref, ob_ref, ga_ref, gb_ref,
                    wa_ref, wb_ref, wo_ref, g_ref, out_ref, nat_ref):
    def natural(ref, slot):
        dil, per = ref.shape[1], ref.shape[2]
        if dil == 1:
            return ref[0, 0].astype(F32)
        for r in range(dil):
            val = ref[0, r].astype(F32)
            for t in range(val.shape[1] // LANES):
                nat_ref.at[slot, t][pl.ds(r, per, stride=dil), :] = val[:, t * LANES:(t + 1) * LANES]
        return jnp.concatenate([nat_ref[slot, t] for t in range(ref.shape[3] // LANES)], axis=1)

    lses = [natural(ref, i) for i, ref in enumerate((l0_ref, l1_ref, l2_ref))]
    outs = [natural(ref, 3 + i) for i, ref in enumerate((o0_ref, o1_ref, o2_ref))]
    top = jnp.maximum(jnp.maximum(lses[0], lses[1]), lses[2])
    es = [jnp.exp2(l - top) for l in lses]
    inv = 1.0 / (es[0] + es[1] + es[2])
    o_a = (es[0] * inv) * outs[0] + (es[1] * inv) * outs[1] + (es[2] * inv) * outs[2]
    y_a = jnp.dot(o_a.astype(BF16), wa_ref[...], preferred_element_type=F32)
    y_b = jnp.dot(ob_ref[...], wb_ref[...], preferred_element_type=F32)
    sig_a = 1.0 / (1.0 + jnp.exp(-ga_ref[...].astype(F32)))
    sig_b = 1.0 / (1.0 + jnp.exp(-gb_ref[...].astype(F32)))
    merged = sig_a * y_a + sig_b * y_b
    z = jnp.dot(merged.astype(BF16), wo_ref[...], preferred_element_type=F32)
    out_ref[...] = x_ref[...] + _rms_norm(z, g_ref[...])


def _mix_out(x, dil_outs, dil_lses, o_moba, gates, w_a, w_b, w_o, g_post, seq, tm):
    n, d = x.shape
    tiles_per_seq = seq // tm
    row = lambda w, col=0: pl.BlockSpec((tm, w), lambda i: (i, col))

    def group_spec(arr):
        dil = arr.shape[1]
        return pl.BlockSpec((1, dil, tm // dil, arr.shape[3]),
                            lambda i: (i // tiles_per_seq, 0, i % tiles_per_seq, 0))

    return pl.pallas_call(
        _mix_out_kernel,
        grid=(n // tm,),
        in_specs=[row(d)] + [group_spec(a) for a in dil_outs] + [group_spec(a) for a in dil_lses]
                 + [row(MOBA_WIDTH), row(d, 0), row(d, 1),
                    _resident(w_a.shape), _resident(w_b.shape), _resident(w_o.shape), _resident((1, d))],
        out_specs=row(d),
        out_shape=jax.ShapeDtypeStruct((n, d), F32),
        scratch_shapes=[pltpu.VMEM((2 * len(DIL_GROUPS), DIL_OUT // LANES, tm, LANES), F32)],
        compiler_params=pltpu.CompilerParams(dimension_semantics=("parallel",),
                                             vmem_limit_bytes=VMEM_LIMIT),
        name="mix_out",
    )(x, *dil_outs, *dil_lses, o_moba, gates, gates, w_a, w_b, w_o, g_post)


CARRY_ROWS = 8


def _ffn_kernel(x_ref, gpre_ref, gpost_ref, wg_ref, wu_ref, cw_ref, cb_ref, wd_ref, out_ref, a_ref,
                *, tiles_per_seq):
    tm = x_ref.shape[0]
    x = x_ref[...]
    h = _rms_norm(x, gpre_ref[...]).astype(BF16)

    @pl.when(pl.program_id(0) % tiles_per_seq == 0)
    def _():
        a_ref[0:CARRY_ROWS, :] = jnp.zeros((CARRY_ROWS, a_ref.shape[1]), F32)

    a_ref[CARRY_ROWS:CARRY_ROWS + tm, :] = jnp.dot(h, wg_ref[...], preferred_element_type=F32)
    up = jnp.dot(h, wu_ref[...], preferred_element_type=F32)
    conv = cb_ref[...]
    for tap in range(CONV_WIDTH):
        back = CONV_WIDTH - 1 - tap
        conv = conv + a_ref[CARRY_ROWS - back:CARRY_ROWS - back + tm, :] * cw_ref[tap:tap + 1, :]
    a_ref[0:CARRY_ROWS, :] = a_ref[tm:tm + CARRY_ROWS, :]
    inner = 0.7978845608028654 * (conv + 0.044715 * (conv * conv * conv))
    u = (0.5 * conv * (1.0 + jnp.tanh(inner))) * up
    z = jnp.dot(u.astype(BF16), wd_ref[...], preferred_element_type=F32)
    out_ref[...] = x + _rms_norm(z, gpost_ref[...])


def _ffn(x, g_pre, g_post, w_gate, w_up, conv_w, conv_b, w_down, seq, tm):
    n, d = x.shape
    d_ff = w_gate.shape[1]
    row = pl.BlockSpec((tm, d), lambda i: (i, 0))
    return pl.pallas_call(
        functools.partial(_ffn_kernel, tiles_per_seq=seq // tm),
        grid=(n // tm,),
        in_specs=[row, _resident((1, d)), _resident((1, d)), _resident(w_gate.shape),
                  _resident(w_up.shape), _resident(conv_w.shape), _resident((1, d_ff)),
                  _resident(w_down.shape)],
        out_specs=row,
        out_shape=jax.ShapeDtypeStruct((n, d), F32),
        scratch_shapes=[pltpu.VMEM((tm + CARRY_ROWS, d_ff), F32)],
        compiler_params=pltpu.CompilerParams(dimension_semantics=("arbitrary",),
                                             vmem_limit_bytes=VMEM_LIMIT),
        name="conv_ffn",
    )(x, g_pre, g_post, w_gate, w_up, conv_w, conv_b, w_down)


def kernel(x, mix_norm_pre, mix_norm_post, w_in, w_branch_dil, w_branch_moba, w_out, ffn_norm_pre, ffn_norm_post, w_ffn_gate, w_ffn_up, ffn_conv_w, ffn_conv_b, w_ffn_down):
    batch, seq, d = x.shape
    depth = w_in.shape[0]
    tm = 512
    idx = jnp.arange(1, N_ATTN_HEADS + 1, dtype=F32)
    slopes = jnp.exp2(-8.0 * idx / N_ATTN_HEADS)
    qkv_dil_cols = 3 * DIL_WIDTH
    qkv_cols = qkv_dil_cols + 3 * MOBA_WIDTH
    col = jnp.arange(w_in.shape[2])
    is_q = (col < DIL_WIDTH) | ((col >= qkv_dil_cols) & (col < qkv_dil_cols + MOBA_WIDTH))
    q_scale = jnp.where(is_q, SCALE * LOG2E, 1.0).astype(F32)

    xf = x.reshape(batch * seq, d)
    for l in range(depth):
        w = (w_in[l] * q_scale).astype(BF16)
        w_groups = [jnp.concatenate([w[:, part * DIL_WIDTH + g * DIL_OUT:part * DIL_WIDTH + (g + 1) * DIL_OUT]
                                     for part in range(3)], axis=1) for g in range(len(DIL_GROUPS))]
        *qkv_groups, qkv_moba, gates = _inproj(
            xf, mix_norm_pre[l][None], w_groups, w[:, qkv_dil_cols:qkv_cols], w[:, qkv_cols:],
            batch, seq, tm)
        dil = [_dilated_attention(slopes, qkv_groups[g], g) for g in range(len(DIL_GROUPS))]
        o_moba = _moba_attention(slopes, qkv_moba, batch, seq)
        xf = _mix_out(xf, [o for o, _ in dil], [s for _, s in dil], o_moba, gates,
                      w_branch_dil[l].astype(BF16), w_branch_moba[l].astype(BF16),
                      w_out[l].astype(BF16), mix_norm_post[l][None], seq, tm)
        xf = _ffn(xf, ffn_norm_pre[l][None], ffn_norm_post[l][None], w_ffn_gate[l].astype(BF16),
                  w_ffn_up[l].astype(BF16), ffn_conv_w[l], ffn_conv_b[l][None],
                  w_ffn_down[l].astype(BF16), seq, tm)
    return xf.reshape(batch, seq, d)
```

```python
import functools

import jax
import jax.numpy as jnp
from jax import lax
from jax.experimental import pallas as pl
from jax.experimental.pallas import tpu as pltpu

F32 = jnp.float32
BF16 = jnp.bfloat16

HEAD_DIM = 64
DIL_GROUPS = ((128, 1), (512, 4), (2048, 16))
DIL_HEADS_PER_GROUP = 4
DIL_HEADS = DIL_HEADS_PER_GROUP * len(DIL_GROUPS)
MOBA_HEADS = 4
MOBA_BLOCK = 256
MOBA_TOPK = 3
N_ATTN_HEADS = DIL_HEADS + MOBA_HEADS
DIL_WIDTH = DIL_HEADS * HEAD_DIM
MOBA_WIDTH = MOBA_HEADS * HEAD_DIM
DIL_OUT = DIL_HEADS_PER_GROUP * HEAD_DIM
DIL_STEPS = 128
RMS_EPS = 1e-6
SCALE = HEAD_DIM ** -0.5
CONV_WIDTH = 3

LANES = 128
HEADS_PER_TILE = LANES // HEAD_DIM
MASKED = -(2.0 ** 100)
VMEM_LIMIT = 56 * 1024 * 1024

LOG2E = 1.4426950408889634
ONES_ROWS = 16

_NT = (((1,), (1,)), ((), ()))


def _emit_pipelined(units, stages, lag):
    done = [dict() for _ in stages]
    for t in range(len(units) + lag * (len(stages) - 1)):
        for k, stage in enumerate(stages):
            u = t - lag * k
            if 0 <= u < len(units):
                prev = done[k - 1].pop(u) if k else ()
                done[k][u] = stage(units[u], *prev)


def _rms_norm(x, g):
    return x * lax.rsqrt(jnp.mean(x * x, axis=-1, keepdims=True) + RMS_EPS) * g


def _resident(shape):
    return pl.BlockSpec(shape, lambda *_: (0,) * len(shape), pipeline_mode=pl.Buffered(1))


def _split3(x):
    hi = x.astype(BF16)
    rem = x - hi.astype(F32)
    mid = rem.astype(BF16)
    lo = (rem - mid.astype(F32)).astype(BF16)
    return hi, mid, lo


def _inproj_kernel(x_ref, g_ref, w_ref, d0_ref, d1_ref, d2_ref, moba_ref, gate_ref):
    tm = x_ref.shape[0]
    h = _rms_norm(x_ref[...], g_ref[...]).astype(BF16)

    def project(col, width):
        return jnp.dot(h, w_ref[:, col:col + width], preferred_element_type=F32)

    def store_tiles(dst, first_tile, res):
        for t in range(res.shape[-1] // LANES):
            dst[..., first_tile + t, :, :] = res[..., t * LANES:(t + 1) * LANES].astype(BF16)

    tiles_per_part = DIL_OUT // LANES
    for g, (o_ref, (_, dil)) in enumerate(zip((d0_ref, d1_ref, d2_ref), DIL_GROUPS)):
        for part in range(3):
            res = project(part * DIL_WIDTH + g * DIL_OUT, DIL_OUT)
            res = jnp.swapaxes(res.reshape(tm // dil, dil, DIL_OUT), 0, 1) if dil > 1 else res[None]
            store_tiles(o_ref.at[0], part * tiles_per_part, res)
    moba0 = 3 * DIL_WIDTH
    store_tiles(moba_ref.at[0], 0, project(moba0, 3 * MOBA_WIDTH))
    gate0 = moba0 + 3 * MOBA_WIDTH
    chunk = 512
    for c in range(0, gate_ref.shape[1], chunk):
        gate_ref[:, c:c + chunk] = project(gate0 + c, chunk).astype(BF16)


def _inproj(x, g, w, batch, seq, tm):
    n, d = x.shape
    tiles_per_seq = seq // tm
    group_w = 3 * DIL_OUT
    moba_w = 3 * MOBA_WIDTH
    gate_w = w.shape[1] - 3 * DIL_WIDTH - moba_w

    group_tiles, moba_tiles = group_w // LANES, moba_w // LANES
    tile_map = lambda i: (i // tiles_per_seq, 0, 0, i % tiles_per_seq, 0)
    dils = [dil for _, dil in DIL_GROUPS]
    return pl.pallas_call(
        _inproj_kernel,
        grid=(n // tm,),
        in_specs=[pl.BlockSpec((tm, d), lambda i: (i, 0)), _resident((1, d)), _resident(w.shape)],
        out_specs=[pl.BlockSpec((1, dil, group_tiles, tm // dil, LANES), tile_map) for dil in dils]
                  + [pl.BlockSpec((1, moba_tiles, tm, LANES), lambda i: (i // tiles_per_seq, 0, i % tiles_per_seq, 0)),
                     pl.BlockSpec((tm, gate_w), lambda i: (i, 0))],
        out_shape=[jax.ShapeDtypeStruct((batch, dil, group_tiles, seq // dil, LANES), BF16) for dil in dils]
                  + [jax.ShapeDtypeStruct((batch, moba_tiles, seq, LANES), BF16),
                     jax.ShapeDtypeStruct((n, gate_w), BF16)],
        compiler_params=pltpu.CompilerParams(dimension_semantics=("parallel",),
                                             vmem_limit_bytes=VMEM_LIMIT),
        name="inproj",
    )(x, g, w)


def _dil_kernel(slopes_ref, q_ref, k_ref, v_ref, o_ref, lse_ref,
                qm_ref, vt_ref, bias_ref, ot_ref, lt_ref, *, dilation, head0):
    jp = pl.program_id(1)
    n_res, sub_len = q_ref.shape[1], q_ref.shape[3]
    blk = DIL_STEPS
    nblk = sub_len // blk

    kk = lax.broadcasted_iota(jnp.int32, (2 * blk, 2 * blk), 0)
    col = lax.broadcasted_iota(jnp.int32, (2 * blk, 2 * blk), 1)
    delta = blk + (col & (blk - 1)) - kk
    slope = jnp.where(col < blk, slopes_ref[head0 + HEADS_PER_TILE * jp],
                      slopes_ref[head0 + HEADS_PER_TILE * jp + 1]) * (float(dilation) * LOG2E)
    bias_ref[...] = jnp.where((delta >= 0) & (delta <= DIL_STEPS), -slope * delta.astype(F32), MASKED)

    lane = lax.broadcasted_iota(jnp.int32, (sub_len, LANES), 1)
    for r in range(n_res):
        q2 = q_ref[0, r, 0]
        for hh in range(HEADS_PER_TILE):
            in_half = (lane >= hh * HEAD_DIM) & (lane < (hh + 1) * HEAD_DIM)
            qm_ref[hh, r] = jnp.where(in_half, q2, jnp.zeros_like(q2))
        vt_ref[r, :LANES, :] = v_ref[0, r, 0].astype(F32).T.astype(BF16)
        vt_ref[r, LANES:, :] = jnp.ones((ONES_ROWS, sub_len), BF16)

    def rows_of(unit):
        r, n = unit
        q_rows = slice(n * blk, (n + 1) * blk)
        return r, q_rows, (slice((n - 1) * blk, (n + 1) * blk) if n else q_rows)

    def scores(unit):
        r, q_rows, k_rows = rows_of(unit)
        queries = jnp.concatenate([qm_ref[0, r, q_rows, :], qm_ref[1, r, q_rows, :]], axis=0)
        s = lax.dot_general(k_ref[0, r, 0, k_rows, :], queries, _NT, preferred_element_type=F32)
        s = s + (bias_ref[...] if unit[1] else bias_ref[blk:, :])
        return s, jnp.max(s, axis=0, keepdims=True)

    def probs(unit, s, m):
        return jnp.exp2(s - m).astype(BF16), m

    def outputs(unit, p, m):
        r, q_rows, k_rows = rows_of(unit)
        o = jnp.dot(vt_ref[r, :, k_rows], p, preferred_element_type=F32)
        l = o[LANES:LANES + 1, :]
        inv = 1.0 / l
        lse = m + jnp.log2(l)
        for hh in range(HEADS_PER_TILE):
            feat = slice(hh * HEAD_DIM, (hh + 1) * HEAD_DIM)
            qcol = slice(hh * blk, (hh + 1) * blk)
            ot_ref[r, feat, q_rows] = o[feat, qcol] * inv[:, qcol]
            lt_ref[r, feat, q_rows] = jnp.broadcast_to(lse[:, qcol], (HEAD_DIM, blk))

    _emit_pipelined([(r, n) for r in range(n_res) for n in range(nblk)], (scores, probs, outputs), lag=2)

    for r in range(n_res):
        o_ref[0, r, 0] = ot_ref[r].T.astype(o_ref.dtype)
        lse_ref[0, r, 0] = lt_ref[r].T


def _dilated_attention(slopes, qkv, group):
    _, dilation = DIL_GROUPS[group]
    batch, n_res, _, sub_len, _ = qkv.shape
    tiles_per_part = DIL_OUT // LANES

    def spec(part):
        return pl.BlockSpec((1, n_res, 1, sub_len, LANES), lambda b, jp: (b, 0, part * tiles_per_part + jp, 0, 0))

    return pl.pallas_call(
        functools.partial(_dil_kernel, dilation=dilation, head0=group * DIL_HEADS_PER_GROUP),
        grid=(batch, tiles_per_part),
        in_specs=[pl.BlockSpec(memory_space=pltpu.SMEM), spec(0), spec(1), spec(2)],
        out_specs=[spec(0), spec(0)],
        out_shape=[jax.ShapeDtypeStruct((batch, n_res, tiles_per_part, sub_len, LANES), BF16),
                   jax.ShapeDtypeStruct((batch, n_res, tiles_per_part, sub_len, LANES), F32)],
        scratch_shapes=[pltpu.VMEM((HEADS_PER_TILE, n_res, sub_len, LANES), BF16),
                        pltpu.VMEM((n_res, LANES + ONES_ROWS, sub_len), BF16),
                        pltpu.VMEM((2 * DIL_STEPS, 2 * DIL_STEPS), F32),
                        pltpu.VMEM((n_res, LANES, sub_len), F32),
                        pltpu.VMEM((n_res, LANES, sub_len), F32)],
        compiler_params=pltpu.CompilerParams(dimension_semantics=("parallel", "parallel"),
                                             vmem_limit_bytes=VMEM_LIMIT),
        name=f"dilated_attn_g{group}",
    )(slopes, qkv, qkv, qkv)


N_MOBA_BLOCKS = 8


def _moba_kernel(slopes_ref, q_ref, k_ref, v_ref, o_ref, qa_ref, ka_ref, vt_ref, ot_ref):
    jp = pl.program_id(1)
    seq = q_ref.shape[2]
    nb = N_MOBA_BLOCKS
    shift = MOBA_BLOCK.bit_length() - 1
    sel_lanes = HEADS_PER_TILE * nb

    q2 = q_ref[0, 0]
    k2 = k_ref[0, 0]
    v_t = v_ref[0, 0].astype(F32).T.astype(BF16)
    for hh in range(HEADS_PER_TILE):
        vt_ref[hh, :HEAD_DIM, :] = v_t[hh * HEAD_DIM:(hh + 1) * HEAD_DIM, :]
        vt_ref[hh, HEAD_DIM:, :] = jnp.ones((ONES_ROWS, seq), BF16)

    lane = lax.broadcasted_iota(jnp.int32, (seq, LANES), 1)
    row = lax.broadcasted_iota(jnp.int32, (seq, LANES), 0)
    one_hot = jnp.where((lane & (nb - 1)) == (row >> shift), 1.0, 0.0)
    in_block = (row & (MOBA_BLOCK - 1)).astype(F32)
    n_one_hot = sel_lanes + 3 * nb
    k_extra = jnp.where(lane < n_one_hot, one_hot, jnp.where(lane < n_one_hot + 3, in_block, 0.0))
    ka_ref[:, :LANES] = k2
    ka_ref[:, LANES:] = k_extra.astype(BF16)

    a_row = lax.broadcasted_iota(jnp.int32, (LANES, seq), 0)
    a_col = lax.broadcasted_iota(jnp.int32, (LANES, seq), 1)
    avg = jnp.where((a_row < sel_lanes) & ((a_row & (nb - 1)) == (a_col >> shift)),
                    1.0 / MOBA_BLOCK, 0.0).astype(BF16)
    km = jnp.dot(avg, k2, preferred_element_type=F32)
    km_row = lax.broadcasted_iota(jnp.int32, (LANES, LANES), 0)
    km_lane = lax.broadcasted_iota(jnp.int32, (LANES, LANES), 1)
    km = jnp.where((km_row < sel_lanes) & ((km_row >> 3) == (km_lane >> 6)), km, 0.0)
    gate_t = sum(lax.dot_general(part, q2, _NT, preferred_element_type=F32) for part in _split3(km))

    blk_idx = lax.broadcasted_iota(jnp.int32, (nb, seq), 0)
    own_blk = lax.broadcasted_iota(jnp.int32, (nb, seq), 1) >> shift
    is_past = blk_idx < own_blk
    sels = []
    for hh in range(HEADS_PER_TILE):
        g = jnp.where(is_past, gate_t[hh * nb:(hh + 1) * nb, :], -jnp.inf)
        cnt = jnp.zeros((nb, seq), F32)
        for m in range(nb):
            tie = jnp.where(blk_idx > m, 1.0, 0.0)
            cnt = cnt + jnp.where(g[m:m + 1, :] > g, 1.0, jnp.where(g[m:m + 1, :] == g, tie, 0.0))
        keep = (is_past & (cnt < MOBA_TOPK)) | (blk_idx == own_blk)
        sels.append(jnp.where(keep, 0.0, MASKED))
    sel_q = jnp.concatenate(sels + [jnp.zeros((LANES - sel_lanes, seq), F32)], axis=0).T

    lane1 = lax.broadcasted_iota(jnp.int32, (1, LANES), 1)
    for hh in range(HEADS_PER_TILE):
        slope = slopes_ref[DIL_HEADS + HEADS_PER_TILE * jp + hh] * LOG2E
        base = jnp.where(lane1 < n_one_hot, slope * float(MOBA_BLOCK) * (lane1 & (nb - 1)).astype(F32), slope)
        term = jnp.where(lane1 < n_one_hot, ((lane1 - sel_lanes) >> 3), lane1 - n_one_hot)
        term = jnp.where(lane1 < sel_lanes, -1, term)
        hi, mid, lo = _split3(base)
        q_const = jnp.where(term == 0, hi, jnp.where(term == 1, mid, jnp.where(term == 2, lo, jnp.zeros_like(hi))))
        own_sel = (lane >= hh * nb) & (lane < (hh + 1) * nb)
        q_extra = jnp.where(own_sel, sel_q.astype(BF16), q_const)
        in_half = (lane >= hh * HEAD_DIM) & (lane < (hh + 1) * HEAD_DIM)
        qa_ref[hh, :, :LANES] = jnp.where(in_half, q2, jnp.zeros_like(q2))
        qa_ref[hh, :, LANES:] = q_extra

    ki = lax.broadcasted_iota(jnp.int32, (MOBA_BLOCK, HEADS_PER_TILE * MOBA_BLOCK), 0)
    qi = lax.broadcasted_iota(jnp.int32, (MOBA_BLOCK, HEADS_PER_TILE * MOBA_BLOCK), 1) & (MOBA_BLOCK - 1)
    causal = jnp.where(ki <= qi, 0.0, MASKED)

    def scores(c):
        start, stop = c * MOBA_BLOCK, (c + 1) * MOBA_BLOCK
        queries = jnp.concatenate([qa_ref[0, start:stop, :], qa_ref[1, start:stop, :]], axis=0)
        s = lax.dot_general(ka_ref[:stop, :], queries, _NT, preferred_element_type=F32)
        s_own = s[start:, :] + causal
        m = jnp.max(s_own, axis=0, keepdims=True)
        if c:
            m = jnp.maximum(m, jnp.max(s[:start, :], axis=0, keepdims=True))
        return (s[:start, :] if c else None), s_own, m

    def probs(c, s_past, s_own, m):
        p_own = jnp.exp2(s_own - m).astype(BF16)
        if not c:
            return (p_own,)
        return (jnp.concatenate([jnp.exp2(s_past - m).astype(BF16), p_own], axis=0),)

    def outputs(c, p):
        start, stop = c * MOBA_BLOCK, (c + 1) * MOBA_BLOCK
        for hh in range(HEADS_PER_TILE):
            o = jnp.dot(vt_ref[hh, :, :stop], p[:, hh * MOBA_BLOCK:(hh + 1) * MOBA_BLOCK],
                        preferred_element_type=F32)
            ot_ref[hh * HEAD_DIM:(hh + 1) * HEAD_DIM, start:stop] = o[:HEAD_DIM] * (1.0 / o[HEAD_DIM:HEAD_DIM + 1])

    _emit_pipelined(list(range(nb)), (scores, probs, outputs), lag=1)
    o_ref[0, 0] = ot_ref[...].T.astype(o_ref.dtype)


def _moba_attention(slopes, qkv):
    batch, _, seq, _ = qkv.shape
    assert seq == N_MOBA_BLOCKS * MOBA_BLOCK
    tiles_per_part = MOBA_WIDTH // LANES
    spec = lambda part: pl.BlockSpec((1, 1, seq, LANES), lambda b, jp: (b, part * tiles_per_part + jp, 0, 0))
    return pl.pallas_call(
        _moba_kernel,
        grid=(batch, tiles_per_part),
        in_specs=[pl.BlockSpec(memory_space=pltpu.SMEM), spec(0), spec(1), spec(2)],
        out_specs=spec(0),
        out_shape=jax.ShapeDtypeStruct((batch, tiles_per_part, seq, LANES), BF16),
        scratch_shapes=[pltpu.VMEM((HEADS_PER_TILE, seq, 2 * LANES), BF16),
                        pltpu.VMEM((seq, 2 * LANES), BF16),
                        pltpu.VMEM((HEADS_PER_TILE, HEAD_DIM + ONES_ROWS, seq), BF16),
                        pltpu.VMEM((LANES, seq), F32)],
        compiler_params=pltpu.CompilerParams(dimension_semantics=("parallel", "parallel"),
                                             vmem_limit_bytes=VMEM_LIMIT),
        name="moba_attn",
    )(slopes, qkv, qkv, qkv)


MIX_SUB_ROWS = 256


def _mix_out_kernel(x_ref, o0_ref, o1_ref, o2_ref, l0_ref, l1_ref, l2_ref, ob_ref, ga_ref, gb_ref,
                    wa_ref, wb_ref, wo_ref, g_ref, out_ref, nat_ref):
    tm = x_ref.shape[0]

    def natural(ref, slot):
        dil, per = ref.shape[1], ref.shape[3]
        for r in range(dil if dil > 1 else 0):
            for t in range(ref.shape[2]):
                nat_ref.at[slot, t][pl.ds(r, per, stride=dil), :] = ref[0, r, t].astype(F32)

    l_refs, o_refs = (l0_ref, l1_ref, l2_ref), (o0_ref, o1_ref, o2_ref)
    for i in range(3):
        natural(l_refs[i], i)
        natural(o_refs[i], 3 + i)

    def rows(j):
        return slice(j * MIX_SUB_ROWS, (j + 1) * MIX_SUB_ROWS)

    def load(ref, slot, j):
        if ref.shape[1] == 1:
            tiles = [ref[0, 0, t, rows(j), :].astype(F32) for t in range(ref.shape[2])]
        else:
            tiles = [nat_ref[slot, t, rows(j), :] for t in range(ref.shape[2])]
        return jnp.concatenate(tiles, axis=1)

    def merge(j):
        lses = [load(l_refs[i], i, j) for i in range(3)]
        outs = [load(o_refs[i], 3 + i, j) for i in range(3)]
        top = jnp.maximum(jnp.maximum(lses[0], lses[1]), lses[2])
        es = [jnp.exp2(l - top) for l in lses]
        inv = 1.0 / (es[0] + es[1] + es[2])
        o_a = (es[0] * inv) * outs[0] + (es[1] * inv) * outs[1] + (es[2] * inv) * outs[2]
        return (o_a.astype(BF16),)

    def branches(j, o_a):
        y_a = jnp.dot(o_a, wa_ref[...], preferred_element_type=F32)
        o_b = jnp.concatenate([ob_ref[0, t, rows(j), :] for t in range(ob_ref.shape[1])], axis=1)
        y_b = jnp.dot(o_b, wb_ref[...], preferred_element_type=F32)
        sig_a = 1.0 / (1.0 + jnp.exp2(ga_ref[rows(j), :].astype(F32) * (-LOG2E)))
        sig_b = 1.0 / (1.0 + jnp.exp2(gb_ref[rows(j), :].astype(F32) * (-LOG2E)))
        return ((sig_a * y_a + sig_b * y_b).astype(BF16),)

    def project(j, merged):
        return (jnp.dot(merged, wo_ref[...], preferred_element_type=F32),)

    def finish(j, z):
        out_ref[rows(j), :] = x_ref[rows(j), :] + _rms_norm(z, g_ref[...])

    _emit_pipelined(list(range(tm // MIX_SUB_ROWS)), (merge, branches, project, finish), lag=1)


def _mix_out(x, dil_outs, dil_lses, o_moba, gates, w_a, w_b, w_o, g_post, seq, tm):
    n, d = x.shape
    tiles_per_seq = seq // tm
    row = lambda w, col=0: pl.BlockSpec((tm, w), lambda i: (i, col))

    def group_spec(arr):
        _, dil, tiles, _, _ = arr.shape
        return pl.BlockSpec((1, dil, tiles, tm // dil, LANES),
                            lambda i: (i // tiles_per_seq, 0, 0, i % tiles_per_seq, 0))

    moba_spec = pl.BlockSpec((1, o_moba.shape[1], tm, LANES), lambda i: (i // tiles_per_seq, 0, i % tiles_per_seq, 0))

    return pl.pallas_call(
        _mix_out_kernel,
        grid=(n // tm,),
        in_specs=[row(d)] + [group_spec(a) for a in dil_outs] + [group_spec(a) for a in dil_lses]
                 + [moba_spec, row(d, 0), row(d, 1),
                    _resident(w_a.shape), _resident(w_b.shape), _resident(w_o.shape), _resident((1, d))],
        out_specs=row(d),
        out_shape=jax.ShapeDtypeStruct((n, d), F32),
        scratch_shapes=[pltpu.VMEM((2 * len(DIL_GROUPS), DIL_OUT // LANES, tm, LANES), F32)],
        compiler_params=pltpu.CompilerParams(dimension_semantics=("parallel",),
                                             vmem_limit_bytes=VMEM_LIMIT),
        name="mix_out",
    )(x, *dil_outs, *dil_lses, o_moba, gates, gates, w_a, w_b, w_o, g_post)


CARRY_ROWS = 8


def _ffn_kernel(x_ref, gpre_ref, gpost_ref, wg_ref, wu_ref, cw_ref, cb_ref, wd_ref, out_ref, a_ref,
                *, tiles_per_seq):
    tm = x_ref.shape[0]
    x = x_ref[...]
    h = _rms_norm(x, gpre_ref[...]).astype(BF16)

    @pl.when(pl.program_id(0) % tiles_per_seq == 0)
    def _():
        a_ref[0:CARRY_ROWS, :] = jnp.zeros((CARRY_ROWS, a_ref.shape[1]), F32)

    a_ref[CARRY_ROWS:CARRY_ROWS + tm, :] = jnp.dot(h, wg_ref[...], preferred_element_type=F32)
    up = jnp.dot(h, wu_ref[...], preferred_element_type=F32)
    conv = cb_ref[...]
    for tap in range(CONV_WIDTH):
        back = CONV_WIDTH - 1 - tap
        conv = conv + a_ref[CARRY_ROWS - back:CARRY_ROWS - back + tm, :] * cw_ref[tap:tap + 1, :]
    a_ref[0:CARRY_ROWS, :] = a_ref[tm:tm + CARRY_ROWS, :]
    inner = 0.7978845608028654 * (conv + 0.044715 * (conv * conv * conv))
    u = (0.5 * conv * (1.0 + jnp.tanh(inner))) * up
    z = jnp.dot(u.astype(BF16), wd_ref[...], preferred_element_type=F32)
    out_ref[...] = x + _rms_norm(z, gpost_ref[...])


def _ffn(x, g_pre, g_post, w_gate, w_up, conv_w, conv_b, w_down, seq, tm):
    n, d = x.shape
    d_ff = w_gate.shape[1]
    row = pl.BlockSpec((tm, d), lambda i: (i, 0))
    return pl.pallas_call(
        functools.partial(_ffn_kernel, tiles_per_seq=seq // tm),
        grid=(n // tm,),
        in_specs=[row, _resident((1, d)), _resident((1, d)), _resident(w_gate.shape),
                  _resident(w_up.shape), _resident(conv_w.shape), _resident((1, d_ff)),
                  _resident(w_down.shape)],
        out_specs=row,
        out_shape=jax.ShapeDtypeStruct((n, d), F32),
        scratch_shapes=[pltpu.VMEM((tm + CARRY_ROWS, d_ff), F32)],
        compiler_params=pltpu.CompilerParams(dimension_semantics=("arbitrary",),
                                             vmem_limit_bytes=VMEM_LIMIT),
        name="conv_ffn",
    )(x, g_pre, g_post, w_gate, w_up, conv_w, conv_b, w_down)


def kernel(x, mix_norm_pre, mix_norm_post, w_in, w_branch_dil, w_branch_moba, w_out, ffn_norm_pre, ffn_norm_post, w_ffn_gate, w_ffn_up, ffn_conv_w, ffn_conv_b, w_ffn_down):
    batch, seq, d = x.shape
    depth = w_in.shape[0]
    tm = 512
    idx = jnp.arange(1, N_ATTN_HEADS + 1, dtype=F32)
    slopes = jnp.exp2(-8.0 * idx / N_ATTN_HEADS)
    qkv_dil_cols = 3 * DIL_WIDTH
    col = jnp.arange(w_in.shape[2])
    is_q = (col < DIL_WIDTH) | ((col >= qkv_dil_cols) & (col < qkv_dil_cols + MOBA_WIDTH))
    q_scale = jnp.where(is_q, SCALE * LOG2E, 1.0).astype(F32)

    xf = x.reshape(batch * seq, d)
    for l in range(depth):
        w = (w_in[l] * q_scale).astype(BF16)
        *qkv_groups, qkv_moba, gates = _inproj(xf, mix_norm_pre[l][None], w, batch, seq, tm)
        dil = [_dilated_attention(slopes, qkv_groups[g], g) for g in range(len(DIL_GROUPS))]
        o_moba = _moba_attention(slopes, qkv_moba)
        xf = _mix_out(xf, [o for o, _ in dil], [s for _, s in dil], o_moba, gates,
                      w_branch_dil[l].astype(BF16), w_branch_moba[l].astype(BF16),
                      w_out[l].astype(BF16), mix_norm_post[l][None], seq, tm)
        xf = _ffn(xf, ffn_norm_pre[l][None], ffn_norm_post[l][None], w_ffn_gate[l].astype(BF16),
                  w_ffn_up[l].astype(BF16), ffn_conv_w[l], ffn_conv_b[l][None],
                  w_ffn_down[l].astype(BF16), seq, tm)
    return xf.reshape(batch, seq, d)
```

```python
import functools

import jax
import jax.numpy as jnp
from jax import lax
from jax.experimental import pallas as pl
from jax.experimental.pallas import tpu as pltpu

F32 = jnp.float32
BF16 = jnp.bfloat16

HEAD_DIM = 64
DIL_GROUPS = ((128, 1), (512, 4), (2048, 16))
DIL_HEADS_PER_GROUP = 4
DIL_HEADS = DIL_HEADS_PER_GROUP * len(DIL_GROUPS)
MOBA_HEADS = 4
MOBA_BLOCK = 256
MOBA_TOPK = 3
N_ATTN_HEADS = DIL_HEADS + MOBA_HEADS
DIL_WIDTH = DIL_HEADS * HEAD_DIM
MOBA_WIDTH = MOBA_HEADS * HEAD_DIM
DIL_OUT = DIL_HEADS_PER_GROUP * HEAD_DIM
DIL_STEPS = 128
RMS_EPS = 1e-6
SCALE = HEAD_DIM ** -0.5
CONV_WIDTH = 3

LANES = 128
HEADS_PER_TILE = LANES // HEAD_DIM
MASKED = -(2.0 ** 100)
VMEM_LIMIT = 56 * 1024 * 1024
MIXER_TILE_ROWS = 1024
FFN_TILE_ROWS = 512

LOG2E = 1.4426950408889634
ONES_ROWS = 16

_NT = (((1,), (1,)), ((), ()))


def _emit_pipelined(units, stages, lag):
    done = [dict() for _ in stages]
    for t in range(len(units) + lag * (len(stages) - 1)):
        for k, stage in enumerate(stages):
            u = t - lag * k
            if 0 <= u < len(units):
                prev = done[k - 1].pop(u) if k else ()
                done[k][u] = stage(units[u], *prev)


def _rms_norm(x, g):
    return x * lax.rsqrt(jnp.mean(x * x, axis=-1, keepdims=True) + RMS_EPS) * g


def _resident(shape):
    return pl.BlockSpec(shape, lambda *_: (0,) * len(shape), pipeline_mode=pl.Buffered(1))


def _split3(x):
    hi = x.astype(BF16)
    rem = x - hi.astype(F32)
    mid = rem.astype(BF16)
    lo = (rem - mid.astype(F32)).astype(BF16)
    return hi, mid, lo


def _inproj_kernel(x_ref, g_ref, w_ref, d0_ref, d1_ref, d2_ref, moba_ref, gate_ref):
    tm = x_ref.shape[0]
    h = _rms_norm(x_ref[...], g_ref[...]).astype(BF16)

    def project(col, width):
        return jnp.dot(h, w_ref[:, col:col + width], preferred_element_type=F32)

    def store_tiles(dst, first_tile, res):
        for t in range(res.shape[-1] // LANES):
            dst[..., first_tile + t, :, :] = res[..., t * LANES:(t + 1) * LANES].astype(BF16)

    tiles_per_part = DIL_OUT // LANES
    for g, (o_ref, (_, dil)) in enumerate(zip((d0_ref, d1_ref, d2_ref), DIL_GROUPS)):
        for part in range(3):
            res = project(part * DIL_WIDTH + g * DIL_OUT, DIL_OUT)
            res = jnp.swapaxes(res.reshape(tm // dil, dil, DIL_OUT), 0, 1) if dil > 1 else res[None]
            store_tiles(o_ref.at[0], part * tiles_per_part, res)
    moba0 = 3 * DIL_WIDTH
    store_tiles(moba_ref.at[0], 0, project(moba0, 3 * MOBA_WIDTH))
    gate0 = moba0 + 3 * MOBA_WIDTH
    chunk = 512
    for c in range(0, gate_ref.shape[1], chunk):
        gate_ref[:, c:c + chunk] = project(gate0 + c, chunk).astype(BF16)


def _inproj(x, g, w, batch, seq, tm):
    n, d = x.shape
    tiles_per_seq = seq // tm
    group_w = 3 * DIL_OUT
    moba_w = 3 * MOBA_WIDTH
    gate_w = w.shape[1] - 3 * DIL_WIDTH - moba_w

    group_tiles, moba_tiles = group_w // LANES, moba_w // LANES
    tile_map = lambda i: (i // tiles_per_seq, 0, 0, i % tiles_per_seq, 0)
    dils = [dil for _, dil in DIL_GROUPS]
    return pl.pallas_call(
        _inproj_kernel,
        grid=(n // tm,),
        in_specs=[pl.BlockSpec((tm, d), lambda i: (i, 0)), _resident((1, d)), _resident(w.shape)],
        out_specs=[pl.BlockSpec((1, dil, group_tiles, tm // dil, LANES), tile_map) for dil in dils]
                  + [pl.BlockSpec((1, moba_tiles, tm, LANES), lambda i: (i // tiles_per_seq, 0, i % tiles_per_seq, 0)),
                     pl.BlockSpec((tm, gate_w), lambda i: (i, 0))],
        out_shape=[jax.ShapeDtypeStruct((batch, dil, group_tiles, seq // dil, LANES), BF16) for dil in dils]
                  + [jax.ShapeDtypeStruct((batch, moba_tiles, seq, LANES), BF16),
                     jax.ShapeDtypeStruct((n, gate_w), BF16)],
        compiler_params=pltpu.CompilerParams(dimension_semantics=("parallel",),
                                             vmem_limit_bytes=VMEM_LIMIT),
        name="inproj",
    )(x, g, w)


def _dil_kernel(slopes_ref, *refs):
    n_groups = len(DIL_GROUPS)
    qkv_refs = [refs[3 * g:3 * g + 3] for g in range(n_groups)]
    out_refs = [refs[3 * n_groups + 2 * g:3 * n_groups + 2 * g + 2] for g in range(n_groups)]
    qm_ref, vt_ref, bias_ref, ot_ref, lt_ref = refs[5 * n_groups:]
    jp = pl.program_id(1)
    blk = DIL_STEPS

    kk = lax.broadcasted_iota(jnp.int32, (2 * blk, 2 * blk), 0)
    col = lax.broadcasted_iota(jnp.int32, (2 * blk, 2 * blk), 1)
    delta = blk + (col & (blk - 1)) - kk
    window = (delta >= 0) & (delta <= DIL_STEPS)

    units = []
    for g, (_, dilation) in enumerate(DIL_GROUPS):
        q_ref, k_ref, v_ref = qkv_refs[g]
        n_res, sub_len = q_ref.shape[1], q_ref.shape[3]
        head0 = g * DIL_HEADS_PER_GROUP
        slope = jnp.where(col < blk, slopes_ref[head0 + HEADS_PER_TILE * jp],
                          slopes_ref[head0 + HEADS_PER_TILE * jp + 1]) * (float(dilation) * LOG2E)
        bias_ref[g] = jnp.where(window, -slope * delta.astype(F32), MASKED)

        lane = lax.broadcasted_iota(jnp.int32, (sub_len, LANES), 1)
        for r in range(n_res):
            q2 = q_ref[0, r, 0]
            rows = slice(r * sub_len, (r + 1) * sub_len)
            for hh in range(HEADS_PER_TILE):
                in_half = (lane >= hh * HEAD_DIM) & (lane < (hh + 1) * HEAD_DIM)
                qm_ref[g, hh, rows, :] = jnp.where(in_half, q2, jnp.zeros_like(q2))
            vt_ref[g, :LANES, rows] = v_ref[0, r, 0].astype(F32).T.astype(BF16)
        vt_ref[g, LANES:, :] = jnp.ones((ONES_ROWS, n_res * sub_len), BF16)
        units += [(g, r, n) for r in range(n_res) for n in range(sub_len // blk)]

    def rows_of(unit):
        g, r, n = unit
        sub_len = qkv_refs[g][0].shape[3]
        q_rows = slice(n * blk, (n + 1) * blk)
        k_rows = slice((n - 1) * blk, (n + 1) * blk) if n else q_rows
        shift = lambda sl: slice(r * sub_len + sl.start, r * sub_len + sl.stop)
        return q_rows, k_rows, shift(q_rows), shift(k_rows)

    def scores(unit):
        g, r, n = unit
        _, k_rows, q_flat, _ = rows_of(unit)
        queries = jnp.concatenate([qm_ref[g, 0, q_flat, :], qm_ref[g, 1, q_flat, :]], axis=0)
        s = lax.dot_general(qkv_refs[g][1][0, r, 0, k_rows, :], queries, _NT, preferred_element_type=F32)
        s = s + (bias_ref[g] if n else bias_ref[g, blk:, :])
        return s, jnp.max(s, axis=0, keepdims=True)

    def probs(unit, s, m):
        return jnp.exp2(s - m).astype(BF16), m

    def outputs(unit, p, m):
        g = unit[0]
        _, _, q_flat, k_flat = rows_of(unit)
        o = jnp.dot(vt_ref[g, :, k_flat], p, preferred_element_type=F32)
        l = o[LANES:LANES + 1, :]
        inv = 1.0 / l
        lse = m + jnp.log2(l)
        for hh in range(HEADS_PER_TILE):
            feat = slice(hh * HEAD_DIM, (hh + 1) * HEAD_DIM)
            qcol = slice(hh * blk, (hh + 1) * blk)
            ot_ref[g, feat, q_flat] = o[feat, qcol] * inv[:, qcol]
            lt_ref[g, feat, q_flat] = jnp.broadcast_to(lse[:, qcol], (HEAD_DIM, blk))

    _emit_pipelined(units, (scores, probs, outputs), lag=2)

    for g in range(n_groups):
        o_ref, lse_ref = out_refs[g]
        n_res, sub_len = o_ref.shape[1], o_ref.shape[3]
        for r in range(n_res):
            rows = slice(r * sub_len, (r + 1) * sub_len)
            o_ref[0, r, 0] = ot_ref[g, :, rows].T.astype(o_ref.dtype)
            lse_ref[0, r, 0] = lt_ref[g, :, rows].T


def _dilated_attention(slopes, qkv_groups):
    batch = qkv_groups[0].shape[0]
    seq = qkv_groups[0].shape[1] * qkv_groups[0].shape[3]
    tiles_per_part = DIL_OUT // LANES
    n_groups = len(qkv_groups)

    def spec(arr, part):
        _, n_res, _, sub_len, _ = arr.shape
        return pl.BlockSpec((1, n_res, 1, sub_len, LANES), lambda b, jp: (b, 0, part * tiles_per_part + jp, 0, 0))

    out_shape, out_specs = [], []
    for arr in qkv_groups:
        _, n_res, _, sub_len, _ = arr.shape
        for dtype in (BF16, F32):
            out_shape.append(jax.ShapeDtypeStruct((batch, n_res, tiles_per_part, sub_len, LANES), dtype))
            out_specs.append(spec(arr, 0))
    outs = pl.pallas_call(
        _dil_kernel,
        grid=(batch, tiles_per_part),
        in_specs=[pl.BlockSpec(memory_space=pltpu.SMEM)] + [spec(arr, part) for arr in qkv_groups for part in range(3)],
        out_specs=out_specs,
        out_shape=out_shape,
        scratch_shapes=[pltpu.VMEM((n_groups, HEADS_PER_TILE, seq, LANES), BF16),
                        pltpu.VMEM((n_groups, LANES + ONES_ROWS, seq), BF16),
                        pltpu.VMEM((n_groups, 2 * DIL_STEPS, 2 * DIL_STEPS), F32),
                        pltpu.VMEM((n_groups, LANES, seq), F32),
                        pltpu.VMEM((n_groups, LANES, seq), F32)],
        compiler_params=pltpu.CompilerParams(dimension_semantics=("parallel", "parallel"),
                                             vmem_limit_bytes=VMEM_LIMIT),
        name="dilated_attn",
    )(slopes, *[arr for arr in qkv_groups for _ in range(3)])
    return outs[0::2], outs[1::2]


N_MOBA_BLOCKS = 8


def _moba_kernel(slopes_ref, q_ref, k_ref, v_ref, o_ref, qa_ref, ka_ref, vt_ref, ot_ref):
    jp = pl.program_id(1)
    seq = q_ref.shape[2]
    nb = N_MOBA_BLOCKS
    shift = MOBA_BLOCK.bit_length() - 1
    sel_lanes = HEADS_PER_TILE * nb

    q2 = q_ref[0, 0]
    k2 = k_ref[0, 0]
    v_t = v_ref[0, 0].astype(F32).T.astype(BF16)
    for hh in range(HEADS_PER_TILE):
        vt_ref[hh, :HEAD_DIM, :] = v_t[hh * HEAD_DIM:(hh + 1) * HEAD_DIM, :]
        vt_ref[hh, HEAD_DIM:, :] = jnp.ones((ONES_ROWS, seq), BF16)

    lane = lax.broadcasted_iota(jnp.int32, (seq, LANES), 1)
    row = lax.broadcasted_iota(jnp.int32, (seq, LANES), 0)
    one_hot = jnp.where((lane & (nb - 1)) == (row >> shift), 1.0, 0.0)
    in_block = (row & (MOBA_BLOCK - 1)).astype(F32)
    n_one_hot = sel_lanes + 3 * nb
    k_extra = jnp.where(lane < n_one_hot, one_hot, jnp.where(lane < n_one_hot + 3, in_block, 0.0))
    ka_ref[:, :LANES] = k2
    ka_ref[:, LANES:] = k_extra.astype(BF16)

    a_row = lax.broadcasted_iota(jnp.int32, (LANES, seq), 0)
    a_col = lax.broadcasted_iota(jnp.int32, (LANES, seq), 1)
    avg = jnp.where((a_row < sel_lanes) & ((a_row & (nb - 1)) == (a_col >> shift)),
                    1.0 / MOBA_BLOCK, 0.0).astype(BF16)
    km = jnp.dot(avg, k2, preferred_element_type=F32)
    km_row = lax.broadcasted_iota(jnp.int32, (LANES, LANES), 0)
    km_lane = lax.broadcasted_iota(jnp.int32, (LANES, LANES), 1)
    km = jnp.where((km_row < sel_lanes) & ((km_row >> 3) == (km_lane >> 6)), km, 0.0)
    gate_t = sum(lax.dot_general(part, q2, _NT, preferred_element_type=F32) for part in _split3(km))

    blk_idx = lax.broadcasted_iota(jnp.int32, (nb, seq), 0)
    own_blk = lax.broadcasted_iota(jnp.int32, (nb, seq), 1) >> shift
    is_past = blk_idx < own_blk
    sels = []
    for hh in range(HEADS_PER_TILE):
        g = jnp.where(is_past, gate_t[hh * nb:(hh + 1) * nb, :], -jnp.inf)
        cnt = jnp.zeros((nb, seq), F32)
        for m in range(nb):
            tie = jnp.where(blk_idx > m, 1.0, 0.0)
            cnt = cnt + jnp.where(g[m:m + 1, :] > g, 1.0, jnp.where(g[m:m + 1, :] == g, tie, 0.0))
        keep = (is_past & (cnt < MOBA_TOPK)) | (blk_idx == own_blk)
        sels.append(jnp.where(keep, 0.0, MASKED))
    sel_q = jnp.concatenate(sels + [jnp.zeros((LANES - sel_lanes, seq), F32)], axis=0).T

    lane1 = lax.broadcasted_iota(jnp.int32, (1, LANES), 1)
    for hh in range(HEADS_PER_TILE):
        slope = slopes_ref[DIL_HEADS + HEADS_PER_TILE * jp + hh] * LOG2E
        base = jnp.where(lane1 < n_one_hot, slope * float(MOBA_BLOCK) * (lane1 & (nb - 1)).astype(F32), slope)
        term = jnp.where(lane1 < n_one_hot, ((lane1 - sel_lanes) >> 3), lane1 - n_one_hot)
        term = jnp.where(lane1 < sel_lanes, -1, term)
        hi, mid, lo = _split3(base)
        q_const = jnp.where(term == 0, hi, jnp.where(term == 1, mid, jnp.where(term == 2, lo, jnp.zeros_like(hi))))
        own_sel = (lane >= hh * nb) & (lane < (hh + 1) * nb)
        q_extra = jnp.where(own_sel, sel_q.astype(BF16), q_const)
        in_half = (lane >= hh * HEAD_DIM) & (lane < (hh + 1) * HEAD_DIM)
        qa_ref[hh, :, :LANES] = jnp.where(in_half, q2, jnp.zeros_like(q2))
        qa_ref[hh, :, LANES:] = q_extra

    ki = lax.broadcasted_iota(jnp.int32, (MOBA_BLOCK, HEADS_PER_TILE * MOBA_BLOCK), 0)
    qi = lax.broadcasted_iota(jnp.int32, (MOBA_BLOCK, HEADS_PER_TILE * MOBA_BLOCK), 1) & (MOBA_BLOCK - 1)
    causal = jnp.where(ki <= qi, 0.0, MASKED)

    def scores(c):
        start, stop = c * MOBA_BLOCK, (c + 1) * MOBA_BLOCK
        queries = jnp.concatenate([qa_ref[0, start:stop, :], qa_ref[1, start:stop, :]], axis=0)
        s = lax.dot_general(ka_ref[:stop, :], queries, _NT, preferred_element_type=F32)
        s_own = s[start:, :] + causal
        m = jnp.max(s_own, axis=0, keepdims=True)
        if c:
            m = jnp.maximum(m, jnp.max(s[:start, :], axis=0, keepdims=True))
        return (s[:start, :] if c else None), s_own, m

    def probs(c, s_past, s_own, m):
        p_own = jnp.exp2(s_own - m).astype(BF16)
        if not c:
            return (p_own,)
        return (jnp.concatenate([jnp.exp2(s_past - m).astype(BF16), p_own], axis=0),)

    def outputs(c, p):
        start, stop = c * MOBA_BLOCK, (c + 1) * MOBA_BLOCK
        for hh in range(HEADS_PER_TILE):
            o = jnp.dot(vt_ref[hh, :, :stop], p[:, hh * MOBA_BLOCK:(hh + 1) * MOBA_BLOCK],
                        preferred_element_type=F32)
            ot_ref[hh * HEAD_DIM:(hh + 1) * HEAD_DIM, start:stop] = o[:HEAD_DIM] * (1.0 / o[HEAD_DIM:HEAD_DIM + 1])

    _emit_pipelined(list(range(nb)), (scores, probs, outputs), lag=1)
    o_ref[0, 0] = ot_ref[...].T.astype(o_ref.dtype)


def _moba_attention(slopes, qkv):
    batch, _, seq, _ = qkv.shape
    assert seq == N_MOBA_BLOCKS * MOBA_BLOCK
    tiles_per_part = MOBA_WIDTH // LANES
    spec = lambda part: pl.BlockSpec((1, 1, seq, LANES), lambda b, jp: (b, part * tiles_per_part + jp, 0, 0))
    return pl.pallas_call(
        _moba_kernel,
        grid=(batch, tiles_per_part),
        in_specs=[pl.BlockSpec(memory_space=pltpu.SMEM), spec(0), spec(1), spec(2)],
        out_specs=spec(0),
        out_shape=jax.ShapeDtypeStruct((batch, tiles_per_part, seq, LANES), BF16),
        scratch_shapes=[pltpu.VMEM((HEADS_PER_TILE, seq, 2 * LANES), BF16),
                        pltpu.VMEM((seq, 2 * LANES), BF16),
                        pltpu.VMEM((HEADS_PER_TILE, HEAD_DIM + ONES_ROWS, seq), BF16),
                        pltpu.VMEM((LANES, seq), F32)],
        compiler_params=pltpu.CompilerParams(dimension_semantics=("parallel", "parallel"),
                                             vmem_limit_bytes=VMEM_LIMIT),
        name="moba_attn",
    )(slopes, qkv, qkv, qkv)


MIX_SUB_ROWS = 256


def _mix_out_kernel(x_ref, o0_ref, o1_ref, o2_ref, l0_ref, l1_ref, l2_ref, ob_ref, ga_ref, gb_ref,
                    wa_ref, wb_ref, wo_ref, g_ref, out_ref, nat_ref):
    tm = x_ref.shape[0]

    def natural(ref, slot):
        dil, per = ref.shape[1], ref.shape[3]
        for r in range(dil if dil > 1 else 0):
            for t in range(ref.shape[2]):
                nat_ref.at[slot, t][pl.ds(r, per, stride=dil), :] = ref[0, r, t].astype(F32)

    l_refs, o_refs = (l0_ref, l1_ref, l2_ref), (o0_ref, o1_ref, o2_ref)
    for i in range(3):
        natural(l_refs[i], i)
        natural(o_refs[i], 3 + i)

    def rows(j):
        return slice(j * MIX_SUB_ROWS, (j + 1) * MIX_SUB_ROWS)

    def load(ref, slot, j):
        if ref.shape[1] == 1:
            tiles = [ref[0, 0, t, rows(j), :].astype(F32) for t in range(ref.shape[2])]
        else:
            tiles = [nat_ref[slot, t, rows(j), :] for t in range(ref.shape[2])]
        return jnp.concatenate(tiles, axis=1)

    def merge(j):
        lses = [load(l_refs[i], i, j) for i in range(3)]
        outs = [load(o_refs[i], 3 + i, j) for i in range(3)]
        top = jnp.maximum(jnp.maximum(lses[0], lses[1]), lses[2])
        es = [jnp.exp2(l - top) for l in lses]
        inv = 1.0 / (es[0] + es[1] + es[2])
        o_a = (es[0] * inv) * outs[0] + (es[1] * inv) * outs[1] + (es[2] * inv) * outs[2]
        return (o_a.astype(BF16),)

    def branches(j, o_a):
        y_a = jnp.dot(o_a, wa_ref[...], preferred_element_type=F32)
        o_b = jnp.concatenate([ob_ref[0, t, rows(j), :] for t in range(ob_ref.shape[1])], axis=1)
        y_b = jnp.dot(o_b, wb_ref[...], preferred_element_type=F32)
        sig_a = 1.0 / (1.0 + jnp.exp2(ga_ref[rows(j), :].astype(F32) * (-LOG2E)))
        sig_b = 1.0 / (1.0 + jnp.exp2(gb_ref[rows(j), :].astype(F32) * (-LOG2E)))
        return ((sig_a * y_a + sig_b * y_b).astype(BF16),)

    def project(j, merged):
        return (jnp.dot(merged, wo_ref[...], preferred_element_type=F32),)

    def finish(j, z):
        out_ref[rows(j), :] = x_ref[rows(j), :] + _rms_norm(z, g_ref[...])

    _emit_pipelined(list(range(tm // MIX_SUB_ROWS)), (merge, branches, project, finish), lag=1)


def _mix_out(x, dil_outs, dil_lses, o_moba, gates, w_a, w_b, w_o, g_post, seq, tm):
    n, d = x.shape
    tiles_per_seq = seq // tm
    row = lambda w, col=0: pl.BlockSpec((tm, w), lambda i: (i, col))

    def group_spec(arr):
        _, dil, tiles, _, _ = arr.shape
        return pl.BlockSpec((1, dil, tiles, tm // dil, LANES),
                            lambda i: (i // tiles_per_seq, 0, 0, i % tiles_per_seq, 0))

    moba_spec = pl.BlockSpec((1, o_moba.shape[1], tm, LANES), lambda i: (i // tiles_per_seq, 0, i % tiles_per_seq, 0))

    return pl.pallas_call(
        _mix_out_kernel,
        grid=(n // tm,),
        in_specs=[row(d)] + [group_spec(a) for a in dil_outs] + [group_spec(a) for a in dil_lses]
                 + [moba_spec, row(d, 0), row(d, 1),
                    _resident(w_a.shape), _resident(w_b.shape), _resident(w_o.shape), _resident((1, d))],
        out_specs=row(d),
        out_shape=jax.ShapeDtypeStruct((n, d), F32),
        scratch_shapes=[pltpu.VMEM((2 * len(DIL_GROUPS), DIL_OUT // LANES, tm, LANES), F32)],
        compiler_params=pltpu.CompilerParams(dimension_semantics=("parallel",),
                                             vmem_limit_bytes=VMEM_LIMIT),
        name="mix_out",
    )(x, *dil_outs, *dil_lses, o_moba, gates, gates, w_a, w_b, w_o, g_post)


CARRY_ROWS = 8


def _ffn_kernel(x_ref, gpre_ref, gpost_ref, wg_ref, wu_ref, cw_ref, cb_ref, wd_ref, out_ref, a_ref,
                *, tiles_per_seq):
    tm = x_ref.shape[0]
    x = x_ref[...]
    h = _rms_norm(x, gpre_ref[...]).astype(BF16)

    @pl.when(pl.program_id(0) % tiles_per_seq == 0)
    def _():
        a_ref[0:CARRY_ROWS, :] = jnp.zeros((CARRY_ROWS, a_ref.shape[1]), F32)

    a_ref[CARRY_ROWS:CARRY_ROWS + tm, :] = jnp.dot(h, wg_ref[...], preferred_element_type=F32)
    up = jnp.dot(h, wu_ref[...], preferred_element_type=F32)
    conv = cb_ref[...]
    for tap in range(CONV_WIDTH):
        back = CONV_WIDTH - 1 - tap
        conv = conv + a_ref[CARRY_ROWS - back:CARRY_ROWS - back + tm, :] * cw_ref[tap:tap + 1, :]
    a_ref[0:CARRY_ROWS, :] = a_ref[tm:tm + CARRY_ROWS, :]
    inner = 0.7978845608028654 * (conv + 0.044715 * (conv * conv * conv))
    u = (0.5 * conv * (1.0 + jnp.tanh(inner))) * up
    z = jnp.dot(u.astype(BF16), wd_ref[...], preferred_element_type=F32)
    out_ref[...] = x + _rms_norm(z, gpost_ref[...])


def _ffn(x, g_pre, g_post, w_gate, w_up, conv_w, conv_b, w_down, seq, tm):
    n, d = x.shape
    d_ff = w_gate.shape[1]
    row = pl.BlockSpec((tm, d), lambda i: (i, 0))
    return pl.pallas_call(
        functools.partial(_ffn_kernel, tiles_per_seq=seq // tm),
        grid=(n // tm,),
        in_specs=[row, _resident((1, d)), _resident((1, d)), _resident(w_gate.shape),
                  _resident(w_up.shape), _resident(conv_w.shape), _resident((1, d_ff)),
                  _resident(w_down.shape)],
        out_specs=row,
        out_shape=jax.ShapeDtypeStruct((n, d), F32),
        scratch_shapes=[pltpu.VMEM((tm + CARRY_ROWS, d_ff), F32)],
        compiler_params=pltpu.CompilerParams(dimension_semantics=("arbitrary",),
                                             vmem_limit_bytes=VMEM_LIMIT),
        name="conv_ffn",
    )(x, g_pre, g_post, w_gate, w_up, conv_w, conv_b, w_down)


def kernel(x, mix_norm_pre, mix_norm_post, w_in, w_branch_dil, w_branch_moba, w_out, ffn_norm_pre, ffn_norm_post, w_ffn_gate, w_ffn_up, ffn_conv_w, ffn_conv_b, w_ffn_down):
    batch, seq, d = x.shape
    depth = w_in.shape[0]
    idx = jnp.arange(1, N_ATTN_HEADS + 1, dtype=F32)
    slopes = jnp.exp2(-8.0 * idx / N_ATTN_HEADS)
    qkv_dil_cols = 3 * DIL_WIDTH
    col = jnp.arange(w_in.shape[2])
    is_q = (col < DIL_WIDTH) | ((col >= qkv_dil_cols) & (col < qkv_dil_cols + MOBA_WIDTH))
    q_scale = jnp.where(is_q, SCALE * LOG2E, 1.0).astype(F32)

    xf = x.reshape(batch * seq, d)
    for l in range(depth):
        w = (w_in[l] * q_scale).astype(BF16)
        *qkv_groups, qkv_moba, gates = _inproj(xf, mix_norm_pre[l][None], w, batch, seq, MIXER_TILE_ROWS)
        dil_outs, dil_lses = _dilated_attention(slopes, qkv_groups)
        o_moba = _moba_attention(slopes, qkv_moba)
        xf = _mix_out(xf, dil_outs, dil_lses, o_moba, gates,
                      w_branch_dil[l].astype(BF16), w_branch_moba[l].astype(BF16),
                      w_out[l].astype(BF16), mix_norm_post[l][None], seq, MIXER_TILE_ROWS)
        xf = _ffn(xf, ffn_norm_pre[l][None], ffn_norm_post[l][None], w_ffn_gate[l].astype(BF16),
                  w_ffn_up[l].astype(BF16), ffn_conv_w[l], ffn_conv_b[l][None],
                  w_ffn_down[l].astype(BF16), seq, FFN_TILE_ROWS)
    return xf.reshape(batch, seq, d)
```

```python
import functools

import jax
import jax.numpy as jnp
from jax import lax
from jax.experimental import pallas as pl
from jax.experimental.pallas import tpu as pltpu

F32 = jnp.float32
BF16 = jnp.bfloat16

HEAD_DIM = 64
DIL_GROUPS = ((128, 1), (512, 4), (2048, 16))
DIL_HEADS_PER_GROUP = 4
DIL_HEADS = DIL_HEADS_PER_GROUP * len(DIL_GROUPS)
MOBA_HEADS = 4
MOBA_BLOCK = 256
MOBA_TOPK = 3
N_ATTN_HEADS = DIL_HEADS + MOBA_HEADS
DIL_WIDTH = DIL_HEADS * HEAD_DIM
MOBA_WIDTH = MOBA_HEADS * HEAD_DIM
DIL_OUT = DIL_HEADS_PER_GROUP * HEAD_DIM
DIL_STEPS = 128
RMS_EPS = 1e-6
SCALE = HEAD_DIM ** -0.5
CONV_WIDTH = 3

LANES = 128
HEADS_PER_TILE = LANES // HEAD_DIM
MASKED = -(2.0 ** 100)
VMEM_LIMIT = 56 * 1024 * 1024
MIXER_TILE_ROWS = 1024
FFN_TILE_ROWS = 512

LOG2E = 1.4426950408889634
ONES_ROWS = 16

_NT = (((1,), (1,)), ((), ()))


def _emit_pipelined(units, stages, lag):
    done = [dict() for _ in stages]
    for t in range(len(units) + lag * (len(stages) - 1)):
        for k, stage in enumerate(stages):
            u = t - lag * k
            if 0 <= u < len(units):
                prev = done[k - 1].pop(u) if k else ()
                done[k][u] = stage(units[u], *prev)


def _rms_norm(x, g):
    return x * lax.rsqrt(jnp.mean(x * x, axis=-1, keepdims=True) + RMS_EPS) * g


def _resident(shape):
    return pl.BlockSpec(shape, lambda *_: (0,) * len(shape), pipeline_mode=pl.Buffered(1))


def _split3(x):
    hi = x.astype(BF16)
    rem = x - hi.astype(F32)
    mid = rem.astype(BF16)
    lo = (rem - mid.astype(F32)).astype(BF16)
    return hi, mid, lo


def _inproj_kernel(x_ref, g_ref, w_ref, d0_ref, d1_ref, d2_ref, moba_ref, gate_ref):
    tm = x_ref.shape[0]
    h = _rms_norm(x_ref[...], g_ref[...]).astype(BF16)

    def project(col, width):
        return jnp.dot(h, w_ref[:, col:col + width], preferred_element_type=F32)

    def store_tiles(dst, first_tile, res):
        for t in range(res.shape[-1] // LANES):
            dst[..., first_tile + t, :, :] = res[..., t * LANES:(t + 1) * LANES].astype(BF16)

    tiles_per_part = DIL_OUT // LANES
    for g, (o_ref, (_, dil)) in enumerate(zip((d0_ref, d1_ref, d2_ref), DIL_GROUPS)):
        for part in range(3):
            res = project(part * DIL_WIDTH + g * DIL_OUT, DIL_OUT)
            res = jnp.swapaxes(res.reshape(tm // dil, dil, DIL_OUT), 0, 1) if dil > 1 else res[None]
            store_tiles(o_ref.at[0], part * tiles_per_part, res)
    moba0 = 3 * DIL_WIDTH
    store_tiles(moba_ref.at[0], 0, project(moba0, 3 * MOBA_WIDTH))
    gate0 = moba0 + 3 * MOBA_WIDTH
    chunk = 512
    for c in range(0, gate_ref.shape[1], chunk):
        gate_ref[:, c:c + chunk] = project(gate0 + c, chunk).astype(BF16)


def _inproj(x, g, w, batch, seq, tm):
    n, d = x.shape
    tiles_per_seq = seq // tm
    group_w = 3 * DIL_OUT
    moba_w = 3 * MOBA_WIDTH
    gate_w = w.shape[1] - 3 * DIL_WIDTH - moba_w

    group_tiles, moba_tiles = group_w // LANES, moba_w // LANES
    tile_map = lambda i: (i // tiles_per_seq, 0, 0, i % tiles_per_seq, 0)
    dils = [dil for _, dil in DIL_GROUPS]
    return pl.pallas_call(
        _inproj_kernel,
        grid=(n // tm,),
        in_specs=[pl.BlockSpec((tm, d), lambda i: (i, 0)), _resident((1, d)), _resident(w.shape)],
        out_specs=[pl.BlockSpec((1, dil, group_tiles, tm // dil, LANES), tile_map) for dil in dils]
                  + [pl.BlockSpec((1, moba_tiles, tm, LANES), lambda i: (i // tiles_per_seq, 0, i % tiles_per_seq, 0)),
                     pl.BlockSpec((tm, gate_w), lambda i: (i, 0))],
        out_shape=[jax.ShapeDtypeStruct((batch, dil, group_tiles, seq // dil, LANES), BF16) for dil in dils]
                  + [jax.ShapeDtypeStruct((batch, moba_tiles, seq, LANES), BF16),
                     jax.ShapeDtypeStruct((n, gate_w), BF16)],
        compiler_params=pltpu.CompilerParams(dimension_semantics=("parallel",),
                                             vmem_limit_bytes=VMEM_LIMIT),
        name="inproj",
    )(x, g, w)


def _dil_kernel(slopes_ref, *refs):
    n_groups = len(DIL_GROUPS)
    qkv_refs = [refs[3 * g:3 * g + 3] for g in range(n_groups)]
    out_refs = [refs[3 * n_groups + 2 * g:3 * n_groups + 2 * g + 2] for g in range(n_groups)]
    qm_ref, vt_ref, bias_ref, ot_ref, lt_ref = refs[5 * n_groups:]
    jp = pl.program_id(1)
    blk = DIL_STEPS

    kk = lax.broadcasted_iota(jnp.int32, (2 * blk, 2 * blk), 0)
    col = lax.broadcasted_iota(jnp.int32, (2 * blk, 2 * blk), 1)
    delta = blk + (col & (blk - 1)) - kk
    window = (delta >= 0) & (delta <= DIL_STEPS)

    units = []
    for g, (_, dilation) in enumerate(DIL_GROUPS):
        q_ref, k_ref, v_ref = qkv_refs[g]
        n_res, sub_len = q_ref.shape[1], q_ref.shape[3]
        head0 = g * DIL_HEADS_PER_GROUP
        slope = jnp.where(col < blk, slopes_ref[head0 + HEADS_PER_TILE * jp],
                          slopes_ref[head0 + HEADS_PER_TILE * jp + 1]) * (float(dilation) * LOG2E)
        bias_ref[g] = jnp.where(window, -slope * delta.astype(F32), MASKED)

        lane = lax.broadcasted_iota(jnp.int32, (sub_len, LANES), 1)
        for r in range(n_res):
            q2 = q_ref[0, r, 0]
            rows = slice(r * sub_len, (r + 1) * sub_len)
            for hh in range(HEADS_PER_TILE):
                in_half = (lane >= hh * HEAD_DIM) & (lane < (hh + 1) * HEAD_DIM)
                qm_ref[g, hh, rows, :] = jnp.where(in_half, q2, jnp.zeros_like(q2))
            vt_ref[g, :LANES, rows] = v_ref[0, r, 0].astype(F32).T.astype(BF16)
        vt_ref[g, LANES:, :] = jnp.ones((ONES_ROWS, n_res * sub_len), BF16)
        units += [(g, r, n) for r in range(n_res) for n in range(sub_len // blk)]

    def rows_of(unit):
        g, r, n = unit
        sub_len = qkv_refs[g][0].shape[3]
        q_rows = slice(n * blk, (n + 1) * blk)
        k_rows = slice((n - 1) * blk, (n + 1) * blk) if n else q_rows
        shift = lambda sl: slice(r * sub_len + sl.start, r * sub_len + sl.stop)
        return q_rows, k_rows, shift(q_rows), shift(k_rows)

    def scores(unit):
        g, r, n = unit
        _, k_rows, q_flat, _ = rows_of(unit)
        queries = jnp.concatenate([qm_ref[g, 0, q_flat, :], qm_ref[g, 1, q_flat, :]], axis=0)
        s = lax.dot_general(qkv_refs[g][1][0, r, 0, k_rows, :], queries, _NT, preferred_element_type=F32)
        s = s + (bias_ref[g] if n else bias_ref[g, blk:, :])
        return s, jnp.max(s, axis=0, keepdims=True)

    def probs(unit, s, m):
        return jnp.exp2(s - m).astype(BF16), m

    def outputs(unit, p, m):
        g = unit[0]
        _, _, q_flat, k_flat = rows_of(unit)
        o = jnp.dot(vt_ref[g, :, k_flat], p, preferred_element_type=F32)
        l = o[LANES:LANES + 1, :]
        inv = 1.0 / l
        lse = m + jnp.log2(l)
        for hh in range(HEADS_PER_TILE):
            feat = slice(hh * HEAD_DIM, (hh + 1) * HEAD_DIM)
            qcol = slice(hh * blk, (hh + 1) * blk)
            ot_ref[g, feat, q_flat] = o[feat, qcol] * inv[:, qcol]
            lt_ref[g, feat, q_flat] = jnp.broadcast_to(lse[:, qcol], (HEAD_DIM, blk))

    _emit_pipelined(units, (scores, probs, outputs), lag=2)

    for g in range(n_groups):
        o_ref, lse_ref = out_refs[g]
        n_res, sub_len = o_ref.shape[1], o_ref.shape[3]
        for r in range(n_res):
            rows = slice(r * sub_len, (r + 1) * sub_len)
            o_ref[0, r, 0] = ot_ref[g, :, rows].T.astype(o_ref.dtype)
            lse_ref[0, r, 0] = lt_ref[g, :, rows].T


def _dilated_attention(slopes, qkv_groups):
    batch = qkv_groups[0].shape[0]
    seq = qkv_groups[0].shape[1] * qkv_groups[0].shape[3]
    tiles_per_part = DIL_OUT // LANES
    n_groups = len(qkv_groups)

    def spec(arr, part):
        _, n_res, _, sub_len, _ = arr.shape
        return pl.BlockSpec((1, n_res, 1, sub_len, LANES), lambda b, jp: (b, 0, part * tiles_per_part + jp, 0, 0))

    out_shape, out_specs = [], []
    for arr in qkv_groups:
        _, n_res, _, sub_len, _ = arr.shape
        for dtype in (BF16, F32):
            out_shape.append(jax.ShapeDtypeStruct((batch, n_res, tiles_per_part, sub_len, LANES), dtype))
            out_specs.append(spec(arr, 0))
    outs = pl.pallas_call(
        _dil_kernel,
        grid=(batch, tiles_per_part),
        in_specs=[pl.BlockSpec(memory_space=pltpu.SMEM)] + [spec(arr, part) for arr in qkv_groups for part in range(3)],
        out_specs=out_specs,
        out_shape=out_shape,
        scratch_shapes=[pltpu.VMEM((n_groups, HEADS_PER_TILE, seq, LANES), BF16),
                        pltpu.VMEM((n_groups, LANES + ONES_ROWS, seq), BF16),
                        pltpu.VMEM((n_groups, 2 * DIL_STEPS, 2 * DIL_STEPS), F32),
                        pltpu.VMEM((n_groups, LANES, seq), F32),
                        pltpu.VMEM((n_groups, LANES, seq), F32)],
        compiler_params=pltpu.CompilerParams(dimension_semantics=("parallel", "parallel"),
                                             vmem_limit_bytes=VMEM_LIMIT),
        name="dilated_attn",
    )(slopes, *[arr for arr in qkv_groups for _ in range(3)])
    return outs[0::2], outs[1::2]


N_MOBA_BLOCKS = 8


def _moba_prepare(jp, slopes_ref, qkv_ref, qa_ref, ka_ref, vt_ref):
    seq = qkv_ref.shape[2]
    n_pairs = MOBA_WIDTH // LANES
    nb = N_MOBA_BLOCKS
    shift = MOBA_BLOCK.bit_length() - 1
    sel_lanes = HEADS_PER_TILE * nb

    q2 = qkv_ref[0, jp]
    k2 = qkv_ref[0, n_pairs + jp]
    v_t = qkv_ref[0, 2 * n_pairs + jp].astype(F32).T.astype(BF16)
    for hh in range(HEADS_PER_TILE):
        vt_ref[jp, hh, :HEAD_DIM, :] = v_t[hh * HEAD_DIM:(hh + 1) * HEAD_DIM, :]
        vt_ref[jp, hh, HEAD_DIM:, :] = jnp.ones((ONES_ROWS, seq), BF16)

    lane = lax.broadcasted_iota(jnp.int32, (seq, LANES), 1)
    row = lax.broadcasted_iota(jnp.int32, (seq, LANES), 0)
    one_hot = jnp.where((lane & (nb - 1)) == (row >> shift), 1.0, 0.0)
    in_block = (row & (MOBA_BLOCK - 1)).astype(F32)
    n_one_hot = sel_lanes + 3 * nb
    k_extra = jnp.where(lane < n_one_hot, one_hot, jnp.where(lane < n_one_hot + 3, in_block, 0.0))
    ka_ref[jp, :, :LANES] = k2
    ka_ref[jp, :, LANES:] = k_extra.astype(BF16)

    a_row = lax.broadcasted_iota(jnp.int32, (LANES, seq), 0)
    a_col = lax.broadcasted_iota(jnp.int32, (LANES, seq), 1)
    avg = jnp.where((a_row < sel_lanes) & ((a_row & (nb - 1)) == (a_col >> shift)),
                    1.0 / MOBA_BLOCK, 0.0).astype(BF16)
    km = jnp.dot(avg, k2, preferred_element_type=F32)
    km_row = lax.broadcasted_iota(jnp.int32, (LANES, LANES), 0)
    km_lane = lax.broadcasted_iota(jnp.int32, (LANES, LANES), 1)
    km = jnp.where((km_row < sel_lanes) & ((km_row >> 3) == (km_lane >> 6)), km, 0.0)
    gate_t = sum(lax.dot_general(part, q2, _NT, preferred_element_type=F32) for part in _split3(km))

    blk_idx = lax.broadcasted_iota(jnp.int32, (nb, seq), 0)
    own_blk = lax.broadcasted_iota(jnp.int32, (nb, seq), 1) >> shift
    is_past = blk_idx < own_blk
    sels = []
    for hh in range(HEADS_PER_TILE):
        g = jnp.where(is_past, gate_t[hh * nb:(hh + 1) * nb, :], -jnp.inf)
        cnt = jnp.zeros((nb, seq), F32)
        for m in range(nb):
            tie = jnp.where(blk_idx > m, 1.0, 0.0)
            cnt = cnt + jnp.where(g[m:m + 1, :] > g, 1.0, jnp.where(g[m:m + 1, :] == g, tie, 0.0))
        keep = (is_past & (cnt < MOBA_TOPK)) | (blk_idx == own_blk)
        sels.append(jnp.where(keep, 0.0, MASKED))
    sel_q = jnp.concatenate(sels + [jnp.zeros((LANES - sel_lanes, seq), F32)], axis=0).T

    lane1 = lax.broadcasted_iota(jnp.int32, (1, LANES), 1)
    for hh in range(HEADS_PER_TILE):
        slope = slopes_ref[DIL_HEADS + HEADS_PER_TILE * jp + hh] * LOG2E
        base = jnp.where(lane1 < n_one_hot, slope * float(MOBA_BLOCK) * (lane1 & (nb - 1)).astype(F32), slope)
        term = jnp.where(lane1 < n_one_hot, ((lane1 - sel_lanes) >> 3), lane1 - n_one_hot)
        term = jnp.where(lane1 < sel_lanes, -1, term)
        hi, mid, lo = _split3(base)
        q_const = jnp.where(term == 0, hi, jnp.where(term == 1, mid, jnp.where(term == 2, lo, jnp.zeros_like(hi))))
        own_sel = (lane >= hh * nb) & (lane < (hh + 1) * nb)
        q_extra = jnp.where(own_sel, sel_q.astype(BF16), q_const)
        in_half = (lane >= hh * HEAD_DIM) & (lane < (hh + 1) * HEAD_DIM)
        qa_ref[jp, hh, :, :LANES] = jnp.where(in_half, q2, jnp.zeros_like(q2))
        qa_ref[jp, hh, :, LANES:] = q_extra


def _moba_kernel(slopes_ref, qkv_ref, o_ref, qa_ref, ka_ref, vt_ref, ot_ref):
    n_pairs = MOBA_WIDTH // LANES
    nb = N_MOBA_BLOCKS
    for jp in range(n_pairs):
        _moba_prepare(jp, slopes_ref, qkv_ref, qa_ref, ka_ref, vt_ref)

    ki = lax.broadcasted_iota(jnp.int32, (MOBA_BLOCK, HEADS_PER_TILE * MOBA_BLOCK), 0)
    qi = lax.broadcasted_iota(jnp.int32, (MOBA_BLOCK, HEADS_PER_TILE * MOBA_BLOCK), 1) & (MOBA_BLOCK - 1)
    causal = jnp.where(ki <= qi, 0.0, MASKED)

    def scores(unit):
        jp, c = unit
        start, stop = c * MOBA_BLOCK, (c + 1) * MOBA_BLOCK
        queries = jnp.concatenate([qa_ref[jp, 0, start:stop, :], qa_ref[jp, 1, start:stop, :]], axis=0)
        s = lax.dot_general(ka_ref[jp, :stop, :], queries, _NT, preferred_element_type=F32)
        s_own = s[start:, :] + causal
        m = jnp.max(s_own, axis=0, keepdims=True)
        if c:
            m = jnp.maximum(m, jnp.max(s[:start, :], axis=0, keepdims=True))
        return (s[:start, :] if c else None), s_own, m

    def probs(unit, s_past, s_own, m):
        p_own = jnp.exp2(s_own - m).astype(BF16)
        if not unit[1]:
            return (p_own,)
        return (jnp.concatenate([jnp.exp2(s_past - m).astype(BF16), p_own], axis=0),)

    def outputs(unit, p):
        jp, c = unit
        start, stop = c * MOBA_BLOCK, (c + 1) * MOBA_BLOCK
        for hh in range(HEADS_PER_TILE):
            o = jnp.dot(vt_ref[jp, hh, :, :stop], p[:, hh * MOBA_BLOCK:(hh + 1) * MOBA_BLOCK],
                        preferred_element_type=F32)
            ot_ref[jp, hh * HEAD_DIM:(hh + 1) * HEAD_DIM, start:stop] = o[:HEAD_DIM] * (1.0 / o[HEAD_DIM:HEAD_DIM + 1])

    _emit_pipelined([(jp, c) for jp in range(n_pairs) for c in range(nb)], (scores, probs, outputs), lag=1)
    for jp in range(n_pairs):
        o_ref[0, jp] = ot_ref[jp].T.astype(o_ref.dtype)


def _moba_attention(slopes, qkv):
    batch, n_tiles, seq, _ = qkv.shape
    assert seq == N_MOBA_BLOCKS * MOBA_BLOCK
    n_pairs = MOBA_WIDTH // LANES
    return pl.pallas_call(
        _moba_kernel,
        grid=(batch,),
        in_specs=[pl.BlockSpec(memory_space=pltpu.SMEM),
                  pl.BlockSpec((1, n_tiles, seq, LANES), lambda b: (b, 0, 0, 0))],
        out_specs=pl.BlockSpec((1, n_pairs, seq, LANES), lambda b: (b, 0, 0, 0)),
        out_shape=jax.ShapeDtypeStruct((batch, n_pairs, seq, LANES), BF16),
        scratch_shapes=[pltpu.VMEM((n_pairs, HEADS_PER_TILE, seq, 2 * LANES), BF16),
                        pltpu.VMEM((n_pairs, seq, 2 * LANES), BF16),
                        pltpu.VMEM((n_pairs, HEADS_PER_TILE, HEAD_DIM + ONES_ROWS, seq), BF16),
                        pltpu.VMEM((n_pairs, LANES, seq), F32)],
        compiler_params=pltpu.CompilerParams(dimension_semantics=("parallel",),
                                             vmem_limit_bytes=VMEM_LIMIT),
        name="moba_attn",
    )(slopes, qkv)


MIX_SUB_ROWS = 256


def _mix_out_kernel(x_ref, o0_ref, o1_ref, o2_ref, l0_ref, l1_ref, l2_ref, ob_ref, ga_ref, gb_ref,
                    wa_ref, wb_ref, wo_ref, g_ref, out_ref, nat_ref):
    tm = x_ref.shape[0]

    def natural(ref, slot):
        dil, per = ref.shape[1], ref.shape[3]
        for r in range(dil if dil > 1 else 0):
            for t in range(ref.shape[2]):
                nat_ref.at[slot, t][pl.ds(r, per, stride=dil), :] = ref[0, r, t].astype(F32)

    l_refs, o_refs = (l0_ref, l1_ref, l2_ref), (o0_ref, o1_ref, o2_ref)
    for i in range(3):
        natural(l_refs[i], i)
        natural(o_refs[i], 3 + i)

    def rows(j):
        return slice(j * MIX_SUB_ROWS, (j + 1) * MIX_SUB_ROWS)

    def load(ref, slot, j):
        if ref.shape[1] == 1:
            tiles = [ref[0, 0, t, rows(j), :].astype(F32) for t in range(ref.shape[2])]
        else:
            tiles = [nat_ref[slot, t, rows(j), :] for t in range(ref.shape[2])]
        return jnp.concatenate(tiles, axis=1)

    def merge(j):
        lses = [load(l_refs[i], i, j) for i in range(3)]
        outs = [load(o_refs[i], 3 + i, j) for i in range(3)]
        top = jnp.maximum(jnp.maximum(lses[0], lses[1]), lses[2])
        es = [jnp.exp2(l - top) for l in lses]
        inv = 1.0 / (es[0] + es[1] + es[2])
        o_a = (es[0] * inv) * outs[0] + (es[1] * inv) * outs[1] + (es[2] * inv) * outs[2]
        return (o_a.astype(BF16),)

    def branches(j, o_a):
        y_a = jnp.dot(o_a, wa_ref[...], preferred_element_type=F32)
        o_b = jnp.concatenate([ob_ref[0, t, rows(j), :] for t in range(ob_ref.shape[1])], axis=1)
        y_b = jnp.dot(o_b, wb_ref[...], preferred_element_type=F32)
        h_a, h_b = 0.5 * y_a, 0.5 * y_b
        t_a = jnp.tanh(0.5 * ga_ref[rows(j), :].astype(F32))
        t_b = jnp.tanh(0.5 * gb_ref[rows(j), :].astype(F32))
        return (((h_a + h_a * t_a) + (h_b + h_b * t_b)).astype(BF16),)

    def project(j, merged):
        return (jnp.dot(merged, wo_ref[...], preferred_element_type=F32),)

    def finish(j, z):
        out_ref[rows(j), :] = x_ref[rows(j), :] + _rms_norm(z, g_ref[...])

    _emit_pipelined(list(range(tm // MIX_SUB_ROWS)), (merge, branches, project, finish), lag=1)


def _mix_out(x, dil_outs, dil_lses, o_moba, gates, w_a, w_b, w_o, g_post, seq, tm):
    n, d = x.shape
    tiles_per_seq = seq // tm
    row = lambda w, col=0: pl.BlockSpec((tm, w), lambda i: (i, col))

    def group_spec(arr):
        _, dil, tiles, _, _ = arr.shape
        return pl.BlockSpec((1, dil, tiles, tm // dil, LANES),
                            lambda i: (i // tiles_per_seq, 0, 0, i % tiles_per_seq, 0))

    moba_spec = pl.BlockSpec((1, o_moba.shape[1], tm, LANES), lambda i: (i // tiles_per_seq, 0, i % tiles_per_seq, 0))

    return pl.pallas_call(
        _mix_out_kernel,
        grid=(n // tm,),
        in_specs=[row(d)] + [group_spec(a) for a in dil_outs] + [group_spec(a) for a in dil_lses]
                 + [moba_spec, row(d, 0), row(d, 1),
                    _resident(w_a.shape), _resident(w_b.shape), _resident(w_o.shape), _resident((1, d))],
        out_specs=row(d),
        out_shape=jax.ShapeDtypeStruct((n, d), F32),
        scratch_shapes=[pltpu.VMEM((2 * len(DIL_GROUPS), DIL_OUT // LANES, tm, LANES), F32)],
        compiler_params=pltpu.CompilerParams(dimension_semantics=("parallel",),
                                             vmem_limit_bytes=VMEM_LIMIT),
        name="mix_out",
    )(x, *dil_outs, *dil_lses, o_moba, gates, gates, w_a, w_b, w_o, g_post)


CARRY_ROWS = 8


def _ffn_kernel(x_ref, gpre_ref, gpost_ref, wg_ref, wu_ref, cw_ref, cb_ref, wd_ref, out_ref, a_ref,
                *, tiles_per_seq):
    tm = x_ref.shape[0]
    x = x_ref[...]
    h = _rms_norm(x, gpre_ref[...]).astype(BF16)

    @pl.when(pl.program_id(0) % tiles_per_seq == 0)
    def _():
        a_ref[0:CARRY_ROWS, :] = jnp.zeros((CARRY_ROWS, a_ref.shape[1]), F32)

    a_ref[CARRY_ROWS:CARRY_ROWS + tm, :] = jnp.dot(h, wg_ref[...], preferred_element_type=F32)
    up = jnp.dot(h, wu_ref[...], preferred_element_type=F32)
    conv = cb_ref[...]
    for tap in range(CONV_WIDTH):
        back = CONV_WIDTH - 1 - tap
        conv = conv + a_ref[CARRY_ROWS - back:CARRY_ROWS - back + tm, :] * cw_ref[tap:tap + 1, :]
    a_ref[0:CARRY_ROWS, :] = a_ref[tm:tm + CARRY_ROWS, :]
    inner = 0.7978845608028654 * (conv + 0.044715 * (conv * conv * conv))
    u = (0.5 * conv * (1.0 + jnp.tanh(inner))) * up
    z = jnp.dot(u.astype(BF16), wd_ref[...], preferred_element_type=F32)
    out_ref[...] = x + _rms_norm(z, gpost_ref[...])


def _ffn(x, g_pre, g_post, w_gate, w_up, conv_w, conv_b, w_down, seq, tm):
    n, d = x.shape
    d_ff = w_gate.shape[1]
    row = pl.BlockSpec((tm, d), lambda i: (i, 0))
    return pl.pallas_call(
        functools.partial(_ffn_kernel, tiles_per_seq=seq // tm),
        grid=(n // tm,),
        in_specs=[row, _resident((1, d)), _resident((1, d)), _resident(w_gate.shape),
                  _resident(w_up.shape), _resident(conv_w.shape), _resident((1, d_ff)),
                  _resident(w_down.shape)],
        out_specs=row,
        out_shape=jax.ShapeDtypeStruct((n, d), F32),
        scratch_shapes=[pltpu.VMEM((tm + CARRY_ROWS, d_ff), F32)],
        compiler_params=pltpu.CompilerParams(dimension_semantics=("arbitrary",),
                                             vmem_limit_bytes=VMEM_LIMIT),
        name="conv_ffn",
    )(x, g_pre, g_post, w_gate, w_up, conv_w, conv_b, w_down)


def kernel(x, mix_norm_pre, mix_norm_post, w_in, w_branch_dil, w_branch_moba, w_out, ffn_norm_pre, ffn_norm_post, w_ffn_gate, w_ffn_up, ffn_conv_w, ffn_conv_b, w_ffn_down):
    batch, seq, d = x.shape
    depth = w_in.shape[0]
    idx = jnp.arange(1, N_ATTN_HEADS + 1, dtype=F32)
    slopes = jnp.exp2(-8.0 * idx / N_ATTN_HEADS)
    qkv_dil_cols = 3 * DIL_WIDTH
    col = jnp.arange(w_in.shape[2])
    is_q = (col < DIL_WIDTH) | ((col >= qkv_dil_cols) & (col < qkv_dil_cols + MOBA_WIDTH))
    q_scale = jnp.where(is_q, SCALE * LOG2E, 1.0).astype(F32)

    xf = x.reshape(batch * seq, d)
    for l in range(depth):
        w = (w_in[l] * q_scale).astype(BF16)
        *qkv_groups, qkv_moba, gates = _inproj(xf, mix_norm_pre[l][None], w, batch, seq, MIXER_TILE_ROWS)
        dil_outs, dil_lses = _dilated_attention(slopes, qkv_groups)
        o_moba = _moba_attention(slopes, qkv_moba)
        xf = _mix_out(xf, dil_outs, dil_lses, o_moba, gates,
                      w_branch_dil[l].astype(BF16), w_branch_moba[l].astype(BF16),
                      w_out[l].astype(BF16), mix_norm_post[l][None], seq, MIXER_TILE_ROWS)
        xf = _ffn(xf, ffn_norm_pre[l][None], ffn_norm_post[l][None], w_ffn_gate[l].astype(BF16),
                  w_ffn_up[l].astype(BF16), ffn_conv_w[l], ffn_conv_b[l][None],
                  w_ffn_down[l].astype(BF16), seq, FFN_TILE_ROWS)
    return xf.reshape(batch, seq, d)
```

```python
import functools

import jax
import jax.numpy as jnp
from jax import lax
from jax.experimental import pallas as pl
from jax.experimental.pallas import tpu as pltpu

F32 = jnp.float32
BF16 = jnp.bfloat16

HEAD_DIM = 64
DIL_GROUPS = ((128, 1), (512, 4), (2048, 16))
DIL_HEADS_PER_GROUP = 4
DIL_HEADS = DIL_HEADS_PER_GROUP * len(DIL_GROUPS)
MOBA_HEADS = 4
MOBA_BLOCK = 256
MOBA_TOPK = 3
N_ATTN_HEADS = DIL_HEADS + MOBA_HEADS
DIL_WIDTH = DIL_HEADS * HEAD_DIM
MOBA_WIDTH = MOBA_HEADS * HEAD_DIM
DIL_OUT = DIL_HEADS_PER_GROUP * HEAD_DIM
DIL_STEPS = 128
RMS_EPS = 1e-6
SCALE = HEAD_DIM ** -0.5
CONV_WIDTH = 3

LANES = 128
HEADS_PER_TILE = LANES // HEAD_DIM
MASKED = -(2.0 ** 100)
VMEM_LIMIT = 56 * 1024 * 1024
MIXER_TILE_ROWS = 1024
FFN_TILE_ROWS = 512

LOG2E = 1.4426950408889634
ONES_ROWS = 16

_NT = (((1,), (1,)), ((), ()))


def _emit_pipelined(units, stages, lag):
    done = [dict() for _ in stages]
    for t in range(len(units) + lag * (len(stages) - 1)):
        for k, stage in enumerate(stages):
            u = t - lag * k
            if 0 <= u < len(units):
                prev = done[k - 1].pop(u) if k else ()
                done[k][u] = stage(units[u], *prev)


def _rms_norm(x, g):
    return x * lax.rsqrt(jnp.mean(x * x, axis=-1, keepdims=True) + RMS_EPS) * g


def _resident(shape):
    return pl.BlockSpec(shape, lambda *_: (0,) * len(shape), pipeline_mode=pl.Buffered(1))


def _split3(x):
    hi = x.astype(BF16)
    rem = x - hi.astype(F32)
    mid = rem.astype(BF16)
    lo = (rem - mid.astype(F32)).astype(BF16)
    return hi, mid, lo


def _inproj_kernel(x_ref, g_ref, w_ref, d0_ref, d1_ref, d2_ref, moba_ref, gate_ref):
    tm = x_ref.shape[0]
    h = _rms_norm(x_ref[...], g_ref[...]).astype(BF16)

    def project(col, width):
        return jnp.dot(h, w_ref[:, col:col + width], preferred_element_type=F32)

    def store_tiles(dst, first_tile, res):
        for t in range(res.shape[-1] // LANES):
            dst[..., first_tile + t, :, :] = res[..., t * LANES:(t + 1) * LANES].astype(BF16)

    tiles_per_part = DIL_OUT // LANES
    for g, (o_ref, (_, dil)) in enumerate(zip((d0_ref, d1_ref, d2_ref), DIL_GROUPS)):
        for part in range(3):
            res = project(part * DIL_WIDTH + g * DIL_OUT, DIL_OUT)
            res = jnp.swapaxes(res.reshape(tm // dil, dil, DIL_OUT), 0, 1) if dil > 1 else res[None]
            store_tiles(o_ref.at[0], part * tiles_per_part, res)
    moba0 = 3 * DIL_WIDTH
    store_tiles(moba_ref.at[0], 0, project(moba0, 3 * MOBA_WIDTH))
    gate0 = moba0 + 3 * MOBA_WIDTH
    chunk = 512
    for c in range(0, gate_ref.shape[1], chunk):
        gate_ref[:, c:c + chunk] = project(gate0 + c, chunk).astype(BF16)


def _inproj(x, g, w, batch, seq, tm):
    n, d = x.shape
    tiles_per_seq = seq // tm
    group_w = 3 * DIL_OUT
    moba_w = 3 * MOBA_WIDTH
    gate_w = w.shape[1] - 3 * DIL_WIDTH - moba_w

    group_tiles, moba_tiles = group_w // LANES, moba_w // LANES
    tile_map = lambda i: (i // tiles_per_seq, 0, 0, i % tiles_per_seq, 0)
    dils = [dil for _, dil in DIL_GROUPS]
    return pl.pallas_call(
        _inproj_kernel,
        grid=(n // tm,),
        in_specs=[pl.BlockSpec((tm, d), lambda i: (i, 0)), _resident((1, d)), _resident(w.shape)],
        out_specs=[pl.BlockSpec((1, dil, group_tiles, tm // dil, LANES), tile_map) for dil in dils]
                  + [pl.BlockSpec((1, moba_tiles, tm, LANES), lambda i: (i // tiles_per_seq, 0, i % tiles_per_seq, 0)),
                     pl.BlockSpec((tm, gate_w), lambda i: (i, 0))],
        out_shape=[jax.ShapeDtypeStruct((batch, dil, group_tiles, seq // dil, LANES), BF16) for dil in dils]
                  + [jax.ShapeDtypeStruct((batch, moba_tiles, seq, LANES), BF16),
                     jax.ShapeDtypeStruct((n, gate_w), BF16)],
        compiler_params=pltpu.CompilerParams(dimension_semantics=("parallel",),
                                             vmem_limit_bytes=VMEM_LIMIT),
        name="inproj",
    )(x, g, w)


def _dil_kernel(slopes_ref, *refs):
    n_groups = len(DIL_GROUPS)
    qkv_refs = [refs[3 * g:3 * g + 3] for g in range(n_groups)]
    out_refs = [refs[3 * n_groups + 2 * g:3 * n_groups + 2 * g + 2] for g in range(n_groups)]
    qm_ref, vt_ref, bias_ref, ot_ref, lt_ref = refs[5 * n_groups:]
    jp = pl.program_id(1)
    blk = DIL_STEPS

    kk = lax.broadcasted_iota(jnp.int32, (2 * blk, 2 * blk), 0)
    col = lax.broadcasted_iota(jnp.int32, (2 * blk, 2 * blk), 1)
    delta = blk + (col & (blk - 1)) - kk
    window = (delta >= 0) & (delta <= DIL_STEPS)

    units = []
    for g, (_, dilation) in enumerate(DIL_GROUPS):
        q_ref, k_ref, v_ref = qkv_refs[g]
        n_res, sub_len = q_ref.shape[1], q_ref.shape[3]
        head0 = g * DIL_HEADS_PER_GROUP
        slope = jnp.where(col < blk, slopes_ref[head0 + HEADS_PER_TILE * jp],
                          slopes_ref[head0 + HEADS_PER_TILE * jp + 1]) * (float(dilation) * LOG2E)
        bias_ref[g] = jnp.where(window, -slope * delta.astype(F32), MASKED)

        feat = lax.broadcasted_iota(jnp.int32, (LANES, sub_len), 0)
        for r in range(n_res):
            q2 = q_ref[0, r, 0]
            rows = slice(r * sub_len, (r + 1) * sub_len)
            q_t = q2.astype(F32).T
            for hh in range(HEADS_PER_TILE):
                in_head = (feat >= hh * HEAD_DIM) & (feat < (hh + 1) * HEAD_DIM)
                qm_ref[g, hh, :, rows] = jnp.where(in_head, q_t, 0.0).astype(BF16)
            vt_ref[g, :LANES, rows] = v_ref[0, r, 0].astype(F32).T.astype(BF16)
        vt_ref[g, LANES:, :] = jnp.ones((ONES_ROWS, n_res * sub_len), BF16)
        units += [(g, r, n) for r in range(n_res) for n in range(sub_len // blk)]

    def rows_of(unit):
        g, r, n = unit
        sub_len = qkv_refs[g][0].shape[3]
        q_rows = slice(n * blk, (n + 1) * blk)
        k_rows = slice((n - 1) * blk, (n + 1) * blk) if n else q_rows
        shift = lambda sl: slice(r * sub_len + sl.start, r * sub_len + sl.stop)
        return q_rows, k_rows, shift(q_rows), shift(k_rows)

    def scores(unit):
        g, r, n = unit
        _, k_rows, q_flat, _ = rows_of(unit)
        queries = jnp.concatenate([qm_ref[g, 0, :, q_flat], qm_ref[g, 1, :, q_flat]], axis=1)
        s = jnp.dot(qkv_refs[g][1][0, r, 0, k_rows, :], queries, preferred_element_type=F32)
        s = s + (bias_ref[g] if n else bias_ref[g, blk:, :])
        return s, jnp.max(s, axis=0, keepdims=True)

    def probs(unit, s, m):
        return jnp.exp2(s - m).astype(BF16), m

    def outputs(unit, p, m):
        g = unit[0]
        _, _, q_flat, k_flat = rows_of(unit)
        o = jnp.dot(vt_ref[g, :, k_flat], p, preferred_element_type=F32)
        l = o[LANES:LANES + 1, :]
        inv = 1.0 / l
        lse = m + jnp.log2(l)
        for hh in range(HEADS_PER_TILE):
            feat = slice(hh * HEAD_DIM, (hh + 1) * HEAD_DIM)
            qcol = slice(hh * blk, (hh + 1) * blk)
            ot_ref[g, feat, q_flat] = o[feat, qcol] * inv[:, qcol]
            lt_ref[g, feat, q_flat] = jnp.broadcast_to(lse[:, qcol], (HEAD_DIM, blk))

    _emit_pipelined(units, (scores, probs, outputs), lag=2)

    for g in range(n_groups):
        o_ref, lse_ref = out_refs[g]
        n_res, sub_len = o_ref.shape[1], o_ref.shape[3]
        for r in range(n_res):
            rows = slice(r * sub_len, (r + 1) * sub_len)
            o_ref[0, r, 0] = ot_ref[g, :, rows].T.astype(o_ref.dtype)
            lse_ref[0, r, 0] = lt_ref[g, :, rows].T


def _dilated_attention(slopes, qkv_groups):
    batch = qkv_groups[0].shape[0]
    seq = qkv_groups[0].shape[1] * qkv_groups[0].shape[3]
    tiles_per_part = DIL_OUT // LANES
    n_groups = len(qkv_groups)

    def spec(arr, part):
        _, n_res, _, sub_len, _ = arr.shape
        return pl.BlockSpec((1, n_res, 1, sub_len, LANES), lambda b, jp: (b, 0, part * tiles_per_part + jp, 0, 0))

    out_shape, out_specs = [], []
    for arr in qkv_groups:
        _, n_res, _, sub_len, _ = arr.shape
        for dtype in (BF16, F32):
            out_shape.append(jax.ShapeDtypeStruct((batch, n_res, tiles_per_part, sub_len, LANES), dtype))
            out_specs.append(spec(arr, 0))
    outs = pl.pallas_call(
        _dil_kernel,
        grid=(batch, tiles_per_part),
        in_specs=[pl.BlockSpec(memory_space=pltpu.SMEM)] + [spec(arr, part) for arr in qkv_groups for part in range(3)],
        out_specs=out_specs,
        out_shape=out_shape,
        scratch_shapes=[pltpu.VMEM((n_groups, HEADS_PER_TILE, LANES, seq), BF16),
                        pltpu.VMEM((n_groups, LANES + ONES_ROWS, seq), BF16),
                        pltpu.VMEM((n_groups, 2 * DIL_STEPS, 2 * DIL_STEPS), F32),
                        pltpu.VMEM((n_groups, LANES, seq), F32),
                        pltpu.VMEM((n_groups, LANES, seq), F32)],
        compiler_params=pltpu.CompilerParams(dimension_semantics=("parallel", "parallel"),
                                             vmem_limit_bytes=VMEM_LIMIT),
        name="dilated_attn",
    )(slopes, *[arr for arr in qkv_groups for _ in range(3)])
    return outs[0::2], outs[1::2]


N_MOBA_BLOCKS = 8


def _moba_prepare(jp, slopes_ref, qkv_ref, qa_ref, ka_ref, vt_ref):
    seq = qkv_ref.shape[2]
    n_pairs = MOBA_WIDTH // LANES
    nb = N_MOBA_BLOCKS
    shift = MOBA_BLOCK.bit_length() - 1
    sel_lanes = HEADS_PER_TILE * nb

    q2 = qkv_ref[0, jp]
    k2 = qkv_ref[0, n_pairs + jp]
    v_t = qkv_ref[0, 2 * n_pairs + jp].astype(F32).T.astype(BF16)
    for hh in range(HEADS_PER_TILE):
        vt_ref[jp, hh, :HEAD_DIM, :] = v_t[hh * HEAD_DIM:(hh + 1) * HEAD_DIM, :]
        vt_ref[jp, hh, HEAD_DIM:, :] = jnp.ones((ONES_ROWS, seq), BF16)

    lane = lax.broadcasted_iota(jnp.int32, (seq, LANES), 1)
    row = lax.broadcasted_iota(jnp.int32, (seq, LANES), 0)
    one_hot = jnp.where((lane & (nb - 1)) == (row >> shift), 1.0, 0.0)
    in_block = (row & (MOBA_BLOCK - 1)).astype(F32)
    n_one_hot = sel_lanes + 3 * nb
    k_extra = jnp.where(lane < n_one_hot, one_hot, jnp.where(lane < n_one_hot + 3, in_block, 0.0))
    ka_ref[jp, :, :LANES] = k2
    ka_ref[jp, :, LANES:] = k_extra.astype(BF16)

    a_row = lax.broadcasted_iota(jnp.int32, (LANES, seq), 0)
    a_col = lax.broadcasted_iota(jnp.int32, (LANES, seq), 1)
    avg = jnp.where((a_row < sel_lanes) & ((a_row & (nb - 1)) == (a_col >> shift)),
                    1.0 / MOBA_BLOCK, 0.0).astype(BF16)
    km = jnp.dot(avg, k2, preferred_element_type=F32)
    km_row = lax.broadcasted_iota(jnp.int32, (LANES, LANES), 0)
    km_lane = lax.broadcasted_iota(jnp.int32, (LANES, LANES), 1)
    km = jnp.where((km_row < sel_lanes) & ((km_row >> 3) == (km_lane >> 6)), km, 0.0)
    gate_t = sum(lax.dot_general(part, q2, _NT, preferred_element_type=F32) for part in _split3(km))

    blk_idx = lax.broadcasted_iota(jnp.int32, (nb, seq), 0)
    own_blk = lax.broadcasted_iota(jnp.int32, (nb, seq), 1) >> shift
    is_past = blk_idx < own_blk
    sels = []
    for hh in range(HEADS_PER_TILE):
        g = jnp.where(is_past, gate_t[hh * nb:(hh + 1) * nb, :], -jnp.inf)
        cnt = jnp.zeros((nb, seq), F32)
        for m in range(nb):
            tie = jnp.where(blk_idx > m, 1.0, 0.0)
            cnt = cnt + jnp.where(g[m:m + 1, :] > g, 1.0, jnp.where(g[m:m + 1, :] == g, tie, 0.0))
        keep = (is_past & (cnt < MOBA_TOPK)) | (blk_idx == own_blk)
        sels.append(jnp.where(keep, 0.0, MASKED))
    sel_t = jnp.concatenate(sels + [jnp.zeros((LANES - sel_lanes, seq), F32)], axis=0)

    q_t = q2.astype(F32).T
    feat = lax.broadcasted_iota(jnp.int32, (LANES, seq), 0)
    lane1 = lax.broadcasted_iota(jnp.int32, (LANES, 1), 0)
    for hh in range(HEADS_PER_TILE):
        slope = slopes_ref[DIL_HEADS + HEADS_PER_TILE * jp + hh] * LOG2E
        base = jnp.where(lane1 < n_one_hot, slope * float(MOBA_BLOCK) * (lane1 & (nb - 1)).astype(F32), slope)
        term = jnp.where(lane1 < n_one_hot, ((lane1 - sel_lanes) >> 3), lane1 - n_one_hot)
        term = jnp.where(lane1 < sel_lanes, -1, term)
        hi, mid, lo = _split3(base)
        q_const = jnp.where(term == 0, hi, jnp.where(term == 1, mid, jnp.where(term == 2, lo, jnp.zeros_like(hi))))
        own_sel = (feat >= hh * nb) & (feat < (hh + 1) * nb)
        in_head = (feat >= hh * HEAD_DIM) & (feat < (hh + 1) * HEAD_DIM)
        qa_ref[jp, hh, :LANES, :] = jnp.where(in_head, q_t, 0.0).astype(BF16)
        qa_ref[jp, hh, LANES:, :] = jnp.where(own_sel, sel_t.astype(BF16), q_const)


def _moba_kernel(slopes_ref, qkv_ref, o_ref, qa_ref, ka_ref, vt_ref, ot_ref):
    n_pairs = MOBA_WIDTH // LANES
    nb = N_MOBA_BLOCKS
    for jp in range(n_pairs):
        _moba_prepare(jp, slopes_ref, qkv_ref, qa_ref, ka_ref, vt_ref)

    ki = lax.broadcasted_iota(jnp.int32, (MOBA_BLOCK, HEADS_PER_TILE * MOBA_BLOCK), 0)
    qi = lax.broadcasted_iota(jnp.int32, (MOBA_BLOCK, HEADS_PER_TILE * MOBA_BLOCK), 1) & (MOBA_BLOCK - 1)
    causal = jnp.where(ki <= qi, 0.0, MASKED)

    def scores(unit):
        jp, c = unit
        start, stop = c * MOBA_BLOCK, (c + 1) * MOBA_BLOCK
        queries = jnp.concatenate([qa_ref[jp, 0, :, start:stop], qa_ref[jp, 1, :, start:stop]], axis=1)
        s = jnp.dot(ka_ref[jp, :stop, :], queries, preferred_element_type=F32)
        s_own = s[start:, :] + causal
        m = jnp.max(s_own, axis=0, keepdims=True)
        if c:
            m = jnp.maximum(m, jnp.max(s[:start, :], axis=0, keepdims=True))
        return (s[:start, :] if c else None), s_own, m

    def probs(unit, s_past, s_own, m):
        p_own = jnp.exp2(s_own - m).astype(BF16)
        if not unit[1]:
            return (p_own,)
        return (jnp.concatenate([jnp.exp2(s_past - m).astype(BF16), p_own], axis=0),)

    def outputs(unit, p):
        jp, c = unit
        start, stop = c * MOBA_BLOCK, (c + 1) * MOBA_BLOCK
        for hh in range(HEADS_PER_TILE):
            o = jnp.dot(vt_ref[jp, hh, :, :stop], p[:, hh * MOBA_BLOCK:(hh + 1) * MOBA_BLOCK],
                        preferred_element_type=F32)
            ot_ref[jp, hh * HEAD_DIM:(hh + 1) * HEAD_DIM, start:stop] = o[:HEAD_DIM] * (1.0 / o[HEAD_DIM:HEAD_DIM + 1])

    _emit_pipelined([(jp, c) for jp in range(n_pairs) for c in range(nb)], (scores, probs, outputs), lag=1)
    for jp in range(n_pairs):
        o_ref[0, jp] = ot_ref[jp].T.astype(o_ref.dtype)


def _moba_attention(slopes, qkv):
    batch, n_tiles, seq, _ = qkv.shape
    assert seq == N_MOBA_BLOCKS * MOBA_BLOCK
    n_pairs = MOBA_WIDTH // LANES
    return pl.pallas_call(
        _moba_kernel,
        grid=(batch,),
        in_specs=[pl.BlockSpec(memory_space=pltpu.SMEM),
                  pl.BlockSpec((1, n_tiles, seq, LANES), lambda b: (b, 0, 0, 0))],
        out_specs=pl.BlockSpec((1, n_pairs, seq, LANES), lambda b: (b, 0, 0, 0)),
        out_shape=jax.ShapeDtypeStruct((batch, n_pairs, seq, LANES), BF16),
        scratch_shapes=[pltpu.VMEM((n_pairs, HEADS_PER_TILE, 2 * LANES, seq), BF16),
                        pltpu.VMEM((n_pairs, seq, 2 * LANES), BF16),
                        pltpu.VMEM((n_pairs, HEADS_PER_TILE, HEAD_DIM + ONES_ROWS, seq), BF16),
                        pltpu.VMEM((n_pairs, LANES, seq), F32)],
        compiler_params=pltpu.CompilerParams(dimension_semantics=("parallel",),
                                             vmem_limit_bytes=VMEM_LIMIT),
        name="moba_attn",
    )(slopes, qkv)


MIX_SUB_ROWS = 256


def _mix_out_kernel(x_ref, o0_ref, o1_ref, o2_ref, l0_ref, l1_ref, l2_ref, ob_ref, ga_ref, gb_ref,
                    wa_ref, wb_ref, wo_ref, g_ref, out_ref, nat_ref):
    tm = x_ref.shape[0]

    def natural(ref, slot):
        dil, per = ref.shape[1], ref.shape[3]
        for r in range(dil if dil > 1 else 0):
            for t in range(ref.shape[2]):
                nat_ref.at[slot, t][pl.ds(r, per, stride=dil), :] = ref[0, r, t].astype(F32)

    l_refs, o_refs = (l0_ref, l1_ref, l2_ref), (o0_ref, o1_ref, o2_ref)
    for i in range(3):
        natural(l_refs[i], i)
        natural(o_refs[i], 3 + i)

    def rows(j):
        return slice(j * MIX_SUB_ROWS, (j + 1) * MIX_SUB_ROWS)

    def load(ref, slot, j):
        if ref.shape[1] == 1:
            tiles = [ref[0, 0, t, rows(j), :].astype(F32) for t in range(ref.shape[2])]
        else:
            tiles = [nat_ref[slot, t, rows(j), :] for t in range(ref.shape[2])]
        return jnp.concatenate(tiles, axis=1)

    def merge(j):
        lses = [load(l_refs[i], i, j) for i in range(3)]
        outs = [load(o_refs[i], 3 + i, j) for i in range(3)]
        top = jnp.maximum(jnp.maximum(lses[0], lses[1]), lses[2])
        es = [jnp.exp2(l - top) for l in lses]
        inv = 1.0 / (es[0] + es[1] + es[2])
        o_a = (es[0] * inv) * outs[0] + (es[1] * inv) * outs[1] + (es[2] * inv) * outs[2]
        return (o_a.astype(BF16),)

    def branches(j, o_a):
        y_a = jnp.dot(o_a, wa_ref[...], preferred_element_type=F32)
        o_b = jnp.concatenate([ob_ref[0, t, rows(j), :] for t in range(ob_ref.shape[1])], axis=1)
        y_b = jnp.dot(o_b, wb_ref[...], preferred_element_type=F32)
        h_a, h_b = 0.5 * y_a, 0.5 * y_b
        t_a = jnp.tanh(0.5 * ga_ref[rows(j), :].astype(F32))
        t_b = jnp.tanh(0.5 * gb_ref[rows(j), :].astype(F32))
        return (((h_a + h_a * t_a) + (h_b + h_b * t_b)).astype(BF16),)

    def project(j, merged):
        return (jnp.dot(merged, wo_ref[...], preferred_element_type=F32),)

    def finish(j, z):
        out_ref[rows(j), :] = x_ref[rows(j), :] + _rms_norm(z, g_ref[...])

    _emit_pipelined(list(range(tm // MIX_SUB_ROWS)), (merge, branches, project, finish), lag=1)


def _mix_out(x, dil_outs, dil_lses, o_moba, gates, w_a, w_b, w_o, g_post, seq, tm):
    n, d = x.shape
    tiles_per_seq = seq // tm
    row = lambda w, col=0: pl.BlockSpec((tm, w), lambda i: (i, col))

    def group_spec(arr):
        _, dil, tiles, _, _ = arr.shape
        return pl.BlockSpec((1, dil, tiles, tm // dil, LANES),
                            lambda i: (i // tiles_per_seq, 0, 0, i % tiles_per_seq, 0))

    moba_spec = pl.BlockSpec((1, o_moba.shape[1], tm, LANES), lambda i: (i // tiles_per_seq, 0, i % tiles_per_seq, 0))

    return pl.pallas_call(
        _mix_out_kernel,
        grid=(n // tm,),
        in_specs=[row(d)] + [group_spec(a) for a in dil_outs] + [group_spec(a) for a in dil_lses]
                 + [moba_spec, row(d, 0), row(d, 1),
                    _resident(w_a.shape), _resident(w_b.shape), _resident(w_o.shape), _resident((1, d))],
        out_specs=row(d),
        out_shape=jax.ShapeDtypeStruct((n, d), F32),
        scratch_shapes=[pltpu.VMEM((2 * len(DIL_GROUPS), DIL_OUT // LANES, tm, LANES), F32)],
        compiler_params=pltpu.CompilerParams(dimension_semantics=("parallel",),
                                             vmem_limit_bytes=VMEM_LIMIT),
        name="mix_out",
    )(x, *dil_outs, *dil_lses, o_moba, gates, gates, w_a, w_b, w_o, g_post)


CARRY_ROWS = 8


def _ffn_kernel(x_ref, gpre_ref, gpost_ref, wg_ref, wu_ref, cw_ref, cb_ref, wd_ref, out_ref, a_ref,
                *, tiles_per_seq):
    tm = x_ref.shape[0]
    x = x_ref[...]
    h = _rms_norm(x, gpre_ref[...]).astype(BF16)

    @pl.when(pl.program_id(0) % tiles_per_seq == 0)
    def _():
        a_ref[0:CARRY_ROWS, :] = jnp.zeros((CARRY_ROWS, a_ref.shape[1]), F32)

    a_ref[CARRY_ROWS:CARRY_ROWS + tm, :] = jnp.dot(h, wg_ref[...], preferred_element_type=F32)
    up = jnp.dot(h, wu_ref[...], preferred_element_type=F32)
    conv = cb_ref[...]
    for tap in range(CONV_WIDTH):
        back = CONV_WIDTH - 1 - tap
        conv = conv + a_ref[CARRY_ROWS - back:CARRY_ROWS - back + tm, :] * cw_ref[tap:tap + 1, :]
    a_ref[0:CARRY_ROWS, :] = a_ref[tm:tm + CARRY_ROWS, :]
    inner = 0.7978845608028654 * (conv + 0.044715 * (conv * conv * conv))
    u = (0.5 * conv * (1.0 + jnp.tanh(inner))) * up
    z = jnp.dot(u.astype(BF16), wd_ref[...], preferred_element_type=F32)
    out_ref[...] = x + _rms_norm(z, gpost_ref[...])


def _ffn(x, g_pre, g_post, w_gate, w_up, conv_w, conv_b, w_down, seq, tm):
    n, d = x.shape
    d_ff = w_gate.shape[1]
    row = pl.BlockSpec((tm, d), lambda i: (i, 0))
    return pl.pallas_call(
        functools.partial(_ffn_kernel, tiles_per_seq=seq // tm),
        grid=(n // tm,),
        in_specs=[row, _resident((1, d)), _resident((1, d)), _resident(w_gate.shape),
                  _resident(w_up.shape), _resident(conv_w.shape), _resident((1, d_ff)),
                  _resident(w_down.shape)],
        out_specs=row,
        out_shape=jax.ShapeDtypeStruct((n, d), F32),
        scratch_shapes=[pltpu.VMEM((tm + CARRY_ROWS, d_ff), F32)],
        compiler_params=pltpu.CompilerParams(dimension_semantics=("arbitrary",),
                                             vmem_limit_bytes=VMEM_LIMIT),
        name="conv_ffn",
    )(x, g_pre, g_post, w_gate, w_up, conv_w, conv_b, w_down)


def kernel(x, mix_norm_pre, mix_norm_post, w_in, w_branch_dil, w_branch_moba, w_out, ffn_norm_pre, ffn_norm_post, w_ffn_gate, w_ffn_up, ffn_conv_w, ffn_conv_b, w_ffn_down):
    batch, seq, d = x.shape
    depth = w_in.shape[0]
    idx = jnp.arange(1, N_ATTN_HEADS + 1, dtype=F32)
    slopes = jnp.exp2(-8.0 * idx / N_ATTN_HEADS)
    qkv_dil_cols = 3 * DIL_WIDTH
    col = jnp.arange(w_in.shape[2])
    is_q = (col < DIL_WIDTH) | ((col >= qkv_dil_cols) & (col < qkv_dil_cols + MOBA_WIDTH))
    q_scale = jnp.where(is_q, SCALE * LOG2E, 1.0).astype(F32)

    xf = x.reshape(batch * seq, d)
    for l in range(depth):
        w = (w_in[l] * q_scale).astype(BF16)
        *qkv_groups, qkv_moba, gates = _inproj(xf, mix_norm_pre[l][None], w, batch, seq, MIXER_TILE_ROWS)
        dil_outs, dil_lses = _dilated_attention(slopes, qkv_groups)
        o_moba = _moba_attention(slopes, qkv_moba)
        xf = _mix_out(xf, dil_outs, dil_lses, o_moba, gates,
                      w_branch_dil[l].astype(BF16), w_branch_moba[l].astype(BF16),
                      w_out[l].astype(BF16), mix_norm_post[l][None], seq, MIXER_TILE_ROWS)
        xf = _ffn(xf, ffn_norm_pre[l][None], ffn_norm_post[l][None], w_ffn_gate[l].astype(BF16),
                  w_ffn_up[l].astype(BF16), ffn_conv_w[l], ffn_conv_b[l][None],
                  w_ffn_down[l].astype(BF16), seq, FFN_TILE_ROWS)
    return xf.reshape(batch, seq, d)
```

```python
import functools

import jax
import jax.numpy as jnp
from jax import lax
from jax.experimental import pallas as pl
from jax.experimental.pallas import tpu as pltpu

F32 = jnp.float32
BF16 = jnp.bfloat16

HEAD_DIM = 64
DIL_GROUPS = ((128, 1), (512, 4), (2048, 16))
DIL_HEADS_PER_GROUP = 4
DIL_HEADS = DIL_HEADS_PER_GROUP * len(DIL_GROUPS)
MOBA_HEADS = 4
MOBA_BLOCK = 256
MOBA_TOPK = 3
N_ATTN_HEADS = DIL_HEADS + MOBA_HEADS
DIL_WIDTH = DIL_HEADS * HEAD_DIM
MOBA_WIDTH = MOBA_HEADS * HEAD_DIM
DIL_OUT = DIL_HEADS_PER_GROUP * HEAD_DIM
DIL_STEPS = 128
RMS_EPS = 1e-6
SCALE = HEAD_DIM ** -0.5
CONV_WIDTH = 3

LANES = 128
HEADS_PER_TILE = LANES // HEAD_DIM
MASKED = -(2.0 ** 100)
VMEM_LIMIT = 56 * 1024 * 1024
MIXER_TILE_ROWS = 1024
FFN_TILE_ROWS = 512

LOG2E = 1.4426950408889634
ONES_ROWS = 16

_NT = (((1,), (1,)), ((), ()))


def _emit_pipelined(units, stages, lag):
    done = [dict() for _ in stages]
    for t in range(len(units) + lag * (len(stages) - 1)):
        for k, stage in enumerate(stages):
            u = t - lag * k
            if 0 <= u < len(units):
                prev = done[k - 1].pop(u) if k else ()
                done[k][u] = stage(units[u], *prev)


def _rms_norm(x, g):
    return x * lax.rsqrt(jnp.mean(x * x, axis=-1, keepdims=True) + RMS_EPS) * g


def _resident(shape):
    return pl.BlockSpec(shape, lambda *_: (0,) * len(shape), pipeline_mode=pl.Buffered(1))


def _split3(x):
    hi = x.astype(BF16)
    rem = x - hi.astype(F32)
    mid = rem.astype(BF16)
    lo = (rem - mid.astype(F32)).astype(BF16)
    return hi, mid, lo


def _inproj_kernel(x_ref, g_ref, w_ref, d0_ref, d1_ref, d2_ref, moba_ref, gate_ref):
    tm = x_ref.shape[0]
    h = _rms_norm(x_ref[...], g_ref[...]).astype(BF16)

    def project(col, width):
        return jnp.dot(h, w_ref[:, col:col + width], preferred_element_type=F32)

    def store_tiles(dst, first_tile, res):
        for t in range(res.shape[-1] // LANES):
            dst[..., first_tile + t, :, :] = res[..., t * LANES:(t + 1) * LANES].astype(BF16)

    tiles_per_part = DIL_OUT // LANES
    for g, (o_ref, (_, dil)) in enumerate(zip((d0_ref, d1_ref, d2_ref), DIL_GROUPS)):
        for part in range(3):
            res = project(part * DIL_WIDTH + g * DIL_OUT, DIL_OUT)
            res = jnp.swapaxes(res.reshape(tm // dil, dil, DIL_OUT), 0, 1) if dil > 1 else res[None]
            store_tiles(o_ref.at[0], part * tiles_per_part, res)
    moba0 = 3 * DIL_WIDTH
    store_tiles(moba_ref.at[0], 0, project(moba0, 3 * MOBA_WIDTH))
    gate0 = moba0 + 3 * MOBA_WIDTH
    chunk = 512
    for c in range(0, gate_ref.shape[1], chunk):
        gate_ref[:, c:c + chunk] = project(gate0 + c, chunk).astype(BF16)


def _inproj(x, g, w, batch, seq, tm):
    n, d = x.shape
    tiles_per_seq = seq // tm
    group_w = 3 * DIL_OUT
    moba_w = 3 * MOBA_WIDTH
    gate_w = w.shape[1] - 3 * DIL_WIDTH - moba_w

    group_tiles, moba_tiles = group_w // LANES, moba_w // LANES
    tile_map = lambda i: (i // tiles_per_seq, 0, 0, i % tiles_per_seq, 0)
    dils = [dil for _, dil in DIL_GROUPS]
    return pl.pallas_call(
        _inproj_kernel,
        grid=(n // tm,),
        in_specs=[pl.BlockSpec((tm, d), lambda i: (i, 0)), _resident((1, d)), _resident(w.shape)],
        out_specs=[pl.BlockSpec((1, dil, group_tiles, tm // dil, LANES), tile_map) for dil in dils]
                  + [pl.BlockSpec((1, moba_tiles, tm, LANES), lambda i: (i // tiles_per_seq, 0, i % tiles_per_seq, 0)),
                     pl.BlockSpec((tm, gate_w), lambda i: (i, 0))],
        out_shape=[jax.ShapeDtypeStruct((batch, dil, group_tiles, seq // dil, LANES), BF16) for dil in dils]
                  + [jax.ShapeDtypeStruct((batch, moba_tiles, seq, LANES), BF16),
                     jax.ShapeDtypeStruct((n, gate_w), BF16)],
        compiler_params=pltpu.CompilerParams(dimension_semantics=("parallel",),
                                             vmem_limit_bytes=VMEM_LIMIT),
        name="inproj",
    )(x, g, w)


def _dil_kernel(slopes_ref, *refs):
    n_groups = len(DIL_GROUPS)
    qkv_refs = [refs[3 * g:3 * g + 3] for g in range(n_groups)]
    out_refs = [refs[3 * n_groups + 2 * g:3 * n_groups + 2 * g + 2] for g in range(n_groups)]
    qm_ref, vt_ref, bias_ref, ot_ref, lt_ref, nat_ref = refs[5 * n_groups:]
    jp = pl.program_id(1)
    blk = DIL_STEPS

    kk = lax.broadcasted_iota(jnp.int32, (2 * blk, 2 * blk), 0)
    col = lax.broadcasted_iota(jnp.int32, (2 * blk, 2 * blk), 1)
    delta = blk + (col & (blk - 1)) - kk
    window = (delta >= 0) & (delta <= DIL_STEPS)

    units = []
    for g, (_, dilation) in enumerate(DIL_GROUPS):
        q_ref, k_ref, v_ref = qkv_refs[g]
        n_res, sub_len = q_ref.shape[1], q_ref.shape[3]
        head0 = g * DIL_HEADS_PER_GROUP
        slope = jnp.where(col < blk, slopes_ref[head0 + HEADS_PER_TILE * jp],
                          slopes_ref[head0 + HEADS_PER_TILE * jp + 1]) * (float(dilation) * LOG2E)
        bias_ref[g] = jnp.where(window, -slope * delta.astype(F32), MASKED)

        lane = lax.broadcasted_iota(jnp.int32, (sub_len, LANES), 1)
        for r in range(n_res):
            q2 = q_ref[0, r, 0]
            rows = slice(r * sub_len, (r + 1) * sub_len)
            for hh in range(HEADS_PER_TILE):
                in_half = (lane >= hh * HEAD_DIM) & (lane < (hh + 1) * HEAD_DIM)
                qm_ref[g, hh, rows, :] = jnp.where(in_half, q2, jnp.zeros_like(q2))
            vt_ref[g, :LANES, rows] = v_ref[0, r, 0].astype(F32).T.astype(BF16)
        vt_ref[g, LANES:, :] = jnp.ones((ONES_ROWS, n_res * sub_len), BF16)
        units += [(g, r, n) for r in range(n_res) for n in range(sub_len // blk)]

    def rows_of(unit):
        g, r, n = unit
        sub_len = qkv_refs[g][0].shape[3]
        q_rows = slice(n * blk, (n + 1) * blk)
        k_rows = slice((n - 1) * blk, (n + 1) * blk) if n else q_rows
        shift = lambda sl: slice(r * sub_len + sl.start, r * sub_len + sl.stop)
        return q_rows, k_rows, shift(q_rows), shift(k_rows)

    def scores(unit):
        g, r, n = unit
        _, k_rows, q_flat, _ = rows_of(unit)
        queries = jnp.concatenate([qm_ref[g, 0, q_flat, :], qm_ref[g, 1, q_flat, :]], axis=0)
        s = lax.dot_general(qkv_refs[g][1][0, r, 0, k_rows, :], queries, _NT, preferred_element_type=F32)
        s = s + (bias_ref[g] if n else bias_ref[g, blk:, :])
        return s, jnp.max(s, axis=0, keepdims=True)

    def probs(unit, s, m):
        return jnp.exp2(s - m).astype(BF16), m

    def outputs(unit, p, m):
        g = unit[0]
        _, _, q_flat, k_flat = rows_of(unit)
        o = jnp.dot(vt_ref[g, :, k_flat], p, preferred_element_type=F32)
        l = o[LANES:LANES + 1, :]
        inv = 1.0 / l
        lse = m + jnp.log2(l)
        for hh in range(HEADS_PER_TILE):
            feat = slice(hh * HEAD_DIM, (hh + 1) * HEAD_DIM)
            qcol = slice(hh * blk, (hh + 1) * blk)
            ot_ref[g, feat, q_flat] = o[feat, qcol] * inv[:, qcol]
            lt_ref[g, feat, q_flat] = jnp.broadcast_to(lse[:, qcol], (HEAD_DIM, blk))

    _emit_pipelined(units, (scores, probs, outputs), lag=2)

    for g, (_, dilation) in enumerate(DIL_GROUPS):
        o_ref, lse_ref = out_refs[g]
        n_res, sub_len = qkv_refs[g][0].shape[1], qkv_refs[g][0].shape[3]
        for r in range(n_res):
            rows = slice(r * sub_len, (r + 1) * sub_len)
            if dilation == 1:
                o_ref[0, 0] = ot_ref[g, :, rows].T.astype(o_ref.dtype)
                lse_ref[0, 0] = lt_ref[g, :, rows].T
            else:
                nat_ref[pl.ds(r, sub_len, stride=dilation), :] = ot_ref[g, :, rows].T
                lse_ref.at[0, 0][pl.ds(r, sub_len, stride=dilation), :] = lt_ref[g, :, rows].T
        if dilation > 1:
            o_ref[0, 0] = nat_ref[...].astype(o_ref.dtype)


def _dilated_attention(slopes, qkv_groups):
    batch = qkv_groups[0].shape[0]
    seq = qkv_groups[0].shape[1] * qkv_groups[0].shape[3]
    tiles_per_part = DIL_OUT // LANES
    n_groups = len(qkv_groups)

    def spec(arr, part):
        _, n_res, _, sub_len, _ = arr.shape
        return pl.BlockSpec((1, n_res, 1, sub_len, LANES), lambda b, jp: (b, 0, part * tiles_per_part + jp, 0, 0))

    nat_spec = pl.BlockSpec((1, 1, seq, LANES), lambda b, jp: (b, jp, 0, 0))
    out_shape, out_specs = [], []
    for arr in qkv_groups:
        for dtype in (BF16, F32):
            out_shape.append(jax.ShapeDtypeStruct((batch, tiles_per_part, seq, LANES), dtype))
            out_specs.append(nat_spec)
    outs = pl.pallas_call(
        _dil_kernel,
        grid=(batch, tiles_per_part),
        in_specs=[pl.BlockSpec(memory_space=pltpu.SMEM)] + [spec(arr, part) for arr in qkv_groups for part in range(3)],
        out_specs=out_specs,
        out_shape=out_shape,
        scratch_shapes=[pltpu.VMEM((n_groups, HEADS_PER_TILE, seq, LANES), BF16),
                        pltpu.VMEM((n_groups, LANES + ONES_ROWS, seq), BF16),
                        pltpu.VMEM((n_groups, 2 * DIL_STEPS, 2 * DIL_STEPS), F32),
                        pltpu.VMEM((n_groups, LANES, seq), F32),
                        pltpu.VMEM((n_groups, LANES, seq), F32),
                        pltpu.VMEM((seq, LANES), F32)],
        compiler_params=pltpu.CompilerParams(dimension_semantics=("parallel", "parallel"),
                                             vmem_limit_bytes=VMEM_LIMIT),
        name="dilated_attn",
    )(slopes, *[arr for arr in qkv_groups for _ in range(3)])
    return outs[0::2], outs[1::2]


N_MOBA_BLOCKS = 8


def _moba_prepare(jp, slopes_ref, qkv_ref, qa_ref, ka_ref, vt_ref):
    seq = qkv_ref.shape[2]
    n_pairs = MOBA_WIDTH // LANES
    nb = N_MOBA_BLOCKS
    shift = MOBA_BLOCK.bit_length() - 1
    sel_lanes = HEADS_PER_TILE * nb

    q2 = qkv_ref[0, jp]
    k2 = qkv_ref[0, n_pairs + jp]
    v_t = qkv_ref[0, 2 * n_pairs + jp].astype(F32).T.astype(BF16)
    for hh in range(HEADS_PER_TILE):
        vt_ref[jp, hh, :HEAD_DIM, :] = v_t[hh * HEAD_DIM:(hh + 1) * HEAD_DIM, :]
        vt_ref[jp, hh, HEAD_DIM:, :] = jnp.ones((ONES_ROWS, seq), BF16)

    lane = lax.broadcasted_iota(jnp.int32, (seq, LANES), 1)
    row = lax.broadcasted_iota(jnp.int32, (seq, LANES), 0)
    one_hot = jnp.where((lane & (nb - 1)) == (row >> shift), 1.0, 0.0)
    in_block = (row & (MOBA_BLOCK - 1)).astype(F32)
    n_one_hot = sel_lanes + 3 * nb
    k_extra = jnp.where(lane < n_one_hot, one_hot, jnp.where(lane < n_one_hot + 3, in_block, 0.0))
    ka_ref[jp, :, :LANES] = k2
    ka_ref[jp, :, LANES:] = k_extra.astype(BF16)

    a_row = lax.broadcasted_iota(jnp.int32, (LANES, seq), 0)
    a_col = lax.broadcasted_iota(jnp.int32, (LANES, seq), 1)
    avg = jnp.where((a_row < sel_lanes) & ((a_row & (nb - 1)) == (a_col >> shift)),
                    1.0 / MOBA_BLOCK, 0.0).astype(BF16)
    km = jnp.dot(avg, k2, preferred_element_type=F32)
    km_row = lax.broadcasted_iota(jnp.int32, (LANES, LANES), 0)
    km_lane = lax.broadcasted_iota(jnp.int32, (LANES, LANES), 1)
    km = jnp.where((km_row < sel_lanes) & ((km_row >> 3) == (km_lane >> 6)), km, 0.0)
    gate_t = sum(lax.dot_general(part, q2, _NT, preferred_element_type=F32) for part in _split3(km))

    blk_idx = lax.broadcasted_iota(jnp.int32, (nb, seq), 0)
    own_blk = lax.broadcasted_iota(jnp.int32, (nb, seq), 1) >> shift
    is_past = blk_idx < own_blk
    sels = []
    for hh in range(HEADS_PER_TILE):
        g = jnp.where(is_past, gate_t[hh * nb:(hh + 1) * nb, :], -jnp.inf)
        cnt = jnp.zeros((nb, seq), F32)
        for m in range(nb):
            tie = jnp.where(blk_idx > m, 1.0, 0.0)
            cnt = cnt + jnp.where(g[m:m + 1, :] > g, 1.0, jnp.where(g[m:m + 1, :] == g, tie, 0.0))
        keep = (is_past & (cnt < MOBA_TOPK)) | (blk_idx == own_blk)
        sels.append(jnp.where(keep, 0.0, MASKED))
    sel_t = jnp.concatenate(sels + [jnp.zeros((LANES - sel_lanes, seq), F32)], axis=0)

    q_t = q2.astype(F32).T
    feat = lax.broadcasted_iota(jnp.int32, (LANES, seq), 0)
    lane1 = lax.broadcasted_iota(jnp.int32, (LANES, 1), 0)
    for hh in range(HEADS_PER_TILE):
        slope = slopes_ref[DIL_HEADS + HEADS_PER_TILE * jp + hh] * LOG2E
        base = jnp.where(lane1 < n_one_hot, slope * float(MOBA_BLOCK) * (lane1 & (nb - 1)).astype(F32), slope)
        term = jnp.where(lane1 < n_one_hot, ((lane1 - sel_lanes) >> 3), lane1 - n_one_hot)
        term = jnp.where(lane1 < sel_lanes, -1, term)
        hi, mid, lo = _split3(base)
        q_const = jnp.where(term == 0, hi, jnp.where(term == 1, mid, jnp.where(term == 2, lo, jnp.zeros_like(hi))))
        own_sel = (feat >= hh * nb) & (feat < (hh + 1) * nb)
        in_head = (feat >= hh * HEAD_DIM) & (feat < (hh + 1) * HEAD_DIM)
        qa_ref[jp, hh, :LANES, :] = jnp.where(in_head, q_t, 0.0).astype(BF16)
        qa_ref[jp, hh, LANES:, :] = jnp.where(own_sel, sel_t.astype(BF16), q_const)


def _moba_kernel(slopes_ref, qkv_ref, o_ref, qa_ref, ka_ref, vt_ref, ot_ref):
    n_pairs = MOBA_WIDTH // LANES
    nb = N_MOBA_BLOCKS
    for jp in range(n_pairs):
        _moba_prepare(jp, slopes_ref, qkv_ref, qa_ref, ka_ref, vt_ref)

    ki = lax.broadcasted_iota(jnp.int32, (MOBA_BLOCK, HEADS_PER_TILE * MOBA_BLOCK), 0)
    qi = lax.broadcasted_iota(jnp.int32, (MOBA_BLOCK, HEADS_PER_TILE * MOBA_BLOCK), 1) & (MOBA_BLOCK - 1)
    causal = jnp.where(ki <= qi, 0.0, MASKED)

    def scores(unit):
        jp, c = unit
        start, stop = c * MOBA_BLOCK, (c + 1) * MOBA_BLOCK
        queries = jnp.concatenate([qa_ref[jp, 0, :, start:stop], qa_ref[jp, 1, :, start:stop]], axis=1)
        s = jnp.dot(ka_ref[jp, :stop, :], queries, preferred_element_type=F32)
        s_own = s[start:, :] + causal
        m = jnp.max(s_own, axis=0, keepdims=True)
        if c:
            m = jnp.maximum(m, jnp.max(s[:start, :], axis=0, keepdims=True))
        return (s[:start, :] if c else None), s_own, m

    def probs(unit, s_past, s_own, m):
        p_own = jnp.exp2(s_own - m).astype(BF16)
        if not unit[1]:
            return (p_own,)
        return (jnp.concatenate([jnp.exp2(s_past - m).astype(BF16), p_own], axis=0),)

    def outputs(unit, p):
        jp, c = unit
        start, stop = c * MOBA_BLOCK, (c + 1) * MOBA_BLOCK
        for hh in range(HEADS_PER_TILE):
            o = jnp.dot(vt_ref[jp, hh, :, :stop], p[:, hh * MOBA_BLOCK:(hh + 1) * MOBA_BLOCK],
                        preferred_element_type=F32)
            ot_ref[jp, hh * HEAD_DIM:(hh + 1) * HEAD_DIM, start:stop] = o[:HEAD_DIM] * (1.0 / o[HEAD_DIM:HEAD_DIM + 1])

    _emit_pipelined([(jp, c) for jp in range(n_pairs) for c in range(nb)], (scores, probs, outputs), lag=1)
    for jp in range(n_pairs):
        o_ref[0, jp] = ot_ref[jp].T.astype(o_ref.dtype)


def _moba_attention(slopes, qkv):
    batch, n_tiles, seq, _ = qkv.shape
    assert seq == N_MOBA_BLOCKS * MOBA_BLOCK
    n_pairs = MOBA_WIDTH // LANES
    return pl.pallas_call(
        _moba_kernel,
        grid=(batch,),
        in_specs=[pl.BlockSpec(memory_space=pltpu.SMEM),
                  pl.BlockSpec((1, n_tiles, seq, LANES), lambda b: (b, 0, 0, 0))],
        out_specs=pl.BlockSpec((1, n_pairs, seq, LANES), lambda b: (b, 0, 0, 0)),
        out_shape=jax.ShapeDtypeStruct((batch, n_pairs, seq, LANES), BF16),
        scratch_shapes=[pltpu.VMEM((n_pairs, HEADS_PER_TILE, 2 * LANES, seq), BF16),
                        pltpu.VMEM((n_pairs, seq, 2 * LANES), BF16),
                        pltpu.VMEM((n_pairs, HEADS_PER_TILE, HEAD_DIM + ONES_ROWS, seq), BF16),
                        pltpu.VMEM((n_pairs, LANES, seq), F32)],
        compiler_params=pltpu.CompilerParams(dimension_semantics=("parallel",),
                                             vmem_limit_bytes=VMEM_LIMIT),
        name="moba_attn",
    )(slopes, qkv)


MIX_SUB_ROWS = 256


def _mix_out_kernel(x_ref, o0_ref, o1_ref, o2_ref, l0_ref, l1_ref, l2_ref, ob_ref, ga_ref, gb_ref,
                    wa_ref, wb_ref, wo_ref, g_ref, out_ref):
    tm = x_ref.shape[0]
    l_refs, o_refs = (l0_ref, l1_ref, l2_ref), (o0_ref, o1_ref, o2_ref)

    def rows(j):
        return slice(j * MIX_SUB_ROWS, (j + 1) * MIX_SUB_ROWS)

    def load(ref, j):
        return jnp.concatenate([ref[0, t, rows(j), :].astype(F32) for t in range(ref.shape[1])], axis=1)

    def merge(j):
        lses = [load(ref, j) for ref in l_refs]
        outs = [load(ref, j) for ref in o_refs]
        top = jnp.maximum(jnp.maximum(lses[0], lses[1]), lses[2])
        es = [jnp.exp2(l - top) for l in lses]
        inv = 1.0 / (es[0] + es[1] + es[2])
        o_a = (es[0] * inv) * outs[0] + (es[1] * inv) * outs[1] + (es[2] * inv) * outs[2]
        return (o_a.astype(BF16),)

    def branches(j, o_a):
        y_a = jnp.dot(o_a, wa_ref[...], preferred_element_type=F32)
        o_b = jnp.concatenate([ob_ref[0, t, rows(j), :] for t in range(ob_ref.shape[1])], axis=1)
        y_b = jnp.dot(o_b, wb_ref[...], preferred_element_type=F32)
        h_a, h_b = 0.5 * y_a, 0.5 * y_b
        t_a = jnp.tanh(0.5 * ga_ref[rows(j), :].astype(F32))
        t_b = jnp.tanh(0.5 * gb_ref[rows(j), :].astype(F32))
        return (((h_a + h_a * t_a) + (h_b + h_b * t_b)).astype(BF16),)

    def project(j, merged):
        return (jnp.dot(merged, wo_ref[...], preferred_element_type=F32),)

    def finish(j, z):
        out_ref[rows(j), :] = x_ref[rows(j), :] + _rms_norm(z, g_ref[...])

    _emit_pipelined(list(range(tm // MIX_SUB_ROWS)), (merge, branches, project, finish), lag=1)


def _mix_out(x, dil_outs, dil_lses, o_moba, gates, w_a, w_b, w_o, g_post, seq, tm):
    n, d = x.shape
    tiles_per_seq = seq // tm
    row = lambda w, col=0: pl.BlockSpec((tm, w), lambda i: (i, col))

    def group_spec(arr):
        return pl.BlockSpec((1, arr.shape[1], tm, LANES), lambda i: (i // tiles_per_seq, 0, i % tiles_per_seq, 0))

    moba_spec = pl.BlockSpec((1, o_moba.shape[1], tm, LANES), lambda i: (i // tiles_per_seq, 0, i % tiles_per_seq, 0))

    return pl.pallas_call(
        _mix_out_kernel,
        grid=(n // tm,),
        in_specs=[row(d)] + [group_spec(a) for a in dil_outs] + [group_spec(a) for a in dil_lses]
                 + [moba_spec, row(d, 0), row(d, 1),
                    _resident(w_a.shape), _resident(w_b.shape), _resident(w_o.shape), _resident((1, d))],
        out_specs=row(d),
        out_shape=jax.ShapeDtypeStruct((n, d), F32),
        compiler_params=pltpu.CompilerParams(dimension_semantics=("parallel",),
                                             vmem_limit_bytes=VMEM_LIMIT),
        name="mix_out",
    )(x, *dil_outs, *dil_lses, o_moba, gates, gates, w_a, w_b, w_o, g_post)


CARRY_ROWS = 8


def _ffn_kernel(x_ref, gpre_ref, gpost_ref, wg_ref, wu_ref, cw_ref, cb_ref, wd_ref, out_ref, a_ref,
                *, tiles_per_seq):
    tm = x_ref.shape[0]
    x = x_ref[...]
    h = _rms_norm(x, gpre_ref[...]).astype(BF16)

    @pl.when(pl.program_id(0) % tiles_per_seq == 0)
    def _():
        a_ref[0:CARRY_ROWS, :] = jnp.zeros((CARRY_ROWS, a_ref.shape[1]), F32)

    a_ref[CARRY_ROWS:CARRY_ROWS + tm, :] = jnp.dot(h, wg_ref[...], preferred_element_type=F32)
    up = jnp.dot(h, wu_ref[...], preferred_element_type=F32)
    conv = cb_ref[...]
    for tap in range(CONV_WIDTH):
        back = CONV_WIDTH - 1 - tap
        conv = conv + a_ref[CARRY_ROWS - back:CARRY_ROWS - back + tm, :] * cw_ref[tap:tap + 1, :]
    a_ref[0:CARRY_ROWS, :] = a_ref[tm:tm + CARRY_ROWS, :]
    inner = 0.7978845608028654 * (conv + 0.044715 * (conv * conv * conv))
    u = (0.5 * conv * (1.0 + jnp.tanh(inner))) * up
    z = jnp.dot(u.astype(BF16), wd_ref[...], preferred_element_type=F32)
    out_ref[...] = x + _rms_norm(z, gpost_ref[...])


def _ffn(x, g_pre, g_post, w_gate, w_up, conv_w, conv_b, w_down, seq, tm):
    n, d = x.shape
    d_ff = w_gate.shape[1]
    row = pl.BlockSpec((tm, d), lambda i: (i, 0))
    return pl.pallas_call(
        functools.partial(_ffn_kernel, tiles_per_seq=seq // tm),
        grid=(n // tm,),
        in_specs=[row, _resident((1, d)), _resident((1, d)), _resident(w_gate.shape),
                  _resident(w_up.shape), _resident(conv_w.shape), _resident((1, d_ff)),
                  _resident(w_down.shape)],
        out_specs=row,
        out_shape=jax.ShapeDtypeStruct((n, d), F32),
        scratch_shapes=[pltpu.VMEM((tm + CARRY_ROWS, d_ff), F32)],
        compiler_params=pltpu.CompilerParams(dimension_semantics=("arbitrary",),
                                             vmem_limit_bytes=VMEM_LIMIT),
        name="conv_ffn",
    )(x, g_pre, g_post, w_gate, w_up, conv_w, conv_b, w_down)


def kernel(x, mix_norm_pre, mix_norm_post, w_in, w_branch_dil, w_branch_moba, w_out, ffn_norm_pre, ffn_norm_post, w_ffn_gate, w_ffn_up, ffn_conv_w, ffn_conv_b, w_ffn_down):
    batch, seq, d = x.shape
    depth = w_in.shape[0]
    idx = jnp.arange(1, N_ATTN_HEADS + 1, dtype=F32)
    slopes = jnp.exp2(-8.0 * idx / N_ATTN_HEADS)
    qkv_dil_cols = 3 * DIL_WIDTH
    col = jnp.arange(w_in.shape[2])
    is_q = (col < DIL_WIDTH) | ((col >= qkv_dil_cols) & (col < qkv_dil_cols + MOBA_WIDTH))
    q_scale = jnp.where(is_q, SCALE * LOG2E, 1.0).astype(F32)

    xf = x.reshape(batch * seq, d)
    for l in range(depth):
        w = (w_in[l] * q_scale).astype(BF16)
        *qkv_groups, qkv_moba, gates = _inproj(xf, mix_norm_pre[l][None], w, batch, seq, MIXER_TILE_ROWS)
        dil_outs, dil_lses = _dilated_attention(slopes, qkv_groups)
        o_moba = _moba_attention(slopes, qkv_moba)
        xf = _mix_out(xf, dil_outs, dil_lses, o_moba, gates,
                      w_branch_dil[l].astype(BF16), w_branch_moba[l].astype(BF16),
                      w_out[l].astype(BF16), mix_norm_post[l][None], seq, MIXER_TILE_ROWS)
        xf = _ffn(xf, ffn_norm_pre[l][None], ffn_norm_post[l][None], w_ffn_gate[l].astype(BF16),
                  w_ffn_up[l].astype(BF16), ffn_conv_w[l], ffn_conv_b[l][None],
                  w_ffn_down[l].astype(BF16), seq, FFN_TILE_ROWS)
    return xf.reshape(batch, seq, d)
```

```python
import functools

import jax
import jax.numpy as jnp
from jax import lax
from jax.experimental import pallas as pl
from jax.experimental.pallas import tpu as pltpu

F32 = jnp.float32
BF16 = jnp.bfloat16

HEAD_DIM = 64
DIL_GROUPS = ((128, 1), (512, 4), (2048, 16))
DIL_HEADS_PER_GROUP = 4
DIL_HEADS = DIL_HEADS_PER_GROUP * len(DIL_GROUPS)
MOBA_HEADS = 4
MOBA_BLOCK = 256
MOBA_TOPK = 3
N_ATTN_HEADS = DIL_HEADS + MOBA_HEADS
DIL_WIDTH = DIL_HEADS * HEAD_DIM
MOBA_WIDTH = MOBA_HEADS * HEAD_DIM
DIL_OUT = DIL_HEADS_PER_GROUP * HEAD_DIM
DIL_STEPS = 128
RMS_EPS = 1e-6
SCALE = HEAD_DIM ** -0.5
CONV_WIDTH = 3

LANES = 128
HEADS_PER_TILE = LANES // HEAD_DIM
MASKED = -(2.0 ** 100)
VMEM_LIMIT = 56 * 1024 * 1024
MIXER_TILE_ROWS = 1024
FFN_TILE_ROWS = 512

LOG2E = 1.4426950408889634
ONES_ROWS = 16

_NT = (((1,), (1,)), ((), ()))


def _emit_pipelined(units, stages, lag):
    done = [dict() for _ in stages]
    for t in range(len(units) + lag * (len(stages) - 1)):
        for k, stage in enumerate(stages):
            u = t - lag * k
            if 0 <= u < len(units):
                prev = done[k - 1].pop(u) if k else ()
                done[k][u] = stage(units[u], *prev)


def _rms_norm(x, g):
    return x * lax.rsqrt(jnp.mean(x * x, axis=-1, keepdims=True) + RMS_EPS) * g


def _resident(shape):
    return pl.BlockSpec(shape, lambda *_: (0,) * len(shape), pipeline_mode=pl.Buffered(1))


def _split3(x):
    hi = x.astype(BF16)
    rem = x - hi.astype(F32)
    mid = rem.astype(BF16)
    lo = (rem - mid.astype(F32)).astype(BF16)
    return hi, mid, lo


def _inproj_kernel(x_ref, g_ref, w_ref, d0_ref, d1_ref, d2_ref, moba_ref, gate_ref):
    tm = x_ref.shape[0]
    h = _rms_norm(x_ref[...], g_ref[...]).astype(BF16)

    def project(col, width):
        return jnp.dot(h, w_ref[:, col:col + width], preferred_element_type=F32)

    def store_tiles(dst, first_tile, res):
        for t in range(res.shape[-1] // LANES):
            dst[..., first_tile + t, :, :] = res[..., t * LANES:(t + 1) * LANES].astype(BF16)

    tiles_per_part = DIL_OUT // LANES
    for g, (o_ref, (_, dil)) in enumerate(zip((d0_ref, d1_ref, d2_ref), DIL_GROUPS)):
        for part in range(3):
            res = project(part * DIL_WIDTH + g * DIL_OUT, DIL_OUT)
            res = jnp.swapaxes(res.reshape(tm // dil, dil, DIL_OUT), 0, 1) if dil > 1 else res[None]
            store_tiles(o_ref.at[0], part * tiles_per_part, res)
    moba0 = 3 * DIL_WIDTH
    store_tiles(moba_ref.at[0], 0, project(moba0, 3 * MOBA_WIDTH))
    gate0 = moba0 + 3 * MOBA_WIDTH
    chunk = 512
    for c in range(0, gate_ref.shape[1], chunk):
        gate_ref[:, c:c + chunk] = project(gate0 + c, chunk).astype(BF16)


def _inproj(x, g, w, batch, seq, tm):
    n, d = x.shape
    tiles_per_seq = seq // tm
    group_w = 3 * DIL_OUT
    moba_w = 3 * MOBA_WIDTH
    gate_w = w.shape[1] - 3 * DIL_WIDTH - moba_w

    group_tiles, moba_tiles = group_w // LANES, moba_w // LANES
    tile_map = lambda i: (i // tiles_per_seq, 0, 0, i % tiles_per_seq, 0)
    dils = [dil for _, dil in DIL_GROUPS]
    return pl.pallas_call(
        _inproj_kernel,
        grid=(n // tm,),
        in_specs=[pl.BlockSpec((tm, d), lambda i: (i, 0)), _resident((1, d)), _resident(w.shape)],
        out_specs=[pl.BlockSpec((1, dil, group_tiles, tm // dil, LANES), tile_map) for dil in dils]
                  + [pl.BlockSpec((1, moba_tiles, tm, LANES), lambda i: (i // tiles_per_seq, 0, i % tiles_per_seq, 0)),
                     pl.BlockSpec((tm, gate_w), lambda i: (i, 0))],
        out_shape=[jax.ShapeDtypeStruct((batch, dil, group_tiles, seq // dil, LANES), BF16) for dil in dils]
                  + [jax.ShapeDtypeStruct((batch, moba_tiles, seq, LANES), BF16),
                     jax.ShapeDtypeStruct((n, gate_w), BF16)],
        compiler_params=pltpu.CompilerParams(dimension_semantics=("parallel",),
                                             vmem_limit_bytes=VMEM_LIMIT),
        name="inproj",
    )(x, g, w)


def _dil_kernel(slopes_ref, *refs):
    n_groups = len(DIL_GROUPS)
    qkv_refs = [refs[3 * g:3 * g + 3] for g in range(n_groups)]
    out_refs = [refs[3 * n_groups + 2 * g:3 * n_groups + 2 * g + 2] for g in range(n_groups)]
    qm_ref, vt_ref, bias_ref, ot_ref, lt_ref, nat_ref = refs[5 * n_groups:]
    jp = pl.program_id(1)
    blk = DIL_STEPS

    kk = lax.broadcasted_iota(jnp.int32, (2 * blk, 2 * blk), 0)
    col = lax.broadcasted_iota(jnp.int32, (2 * blk, 2 * blk), 1)
    delta = blk + (col & (blk - 1)) - kk
    window = (delta >= 0) & (delta <= DIL_STEPS)

    units = []
    for g, (_, dilation) in enumerate(DIL_GROUPS):
        q_ref, k_ref, v_ref = qkv_refs[g]
        n_res, sub_len = q_ref.shape[1], q_ref.shape[3]
        head0 = g * DIL_HEADS_PER_GROUP
        slope = jnp.where(col < blk, slopes_ref[head0 + HEADS_PER_TILE * jp],
                          slopes_ref[head0 + HEADS_PER_TILE * jp + 1]) * (float(dilation) * LOG2E)
        bias_ref[g] = jnp.where(window, -slope * delta.astype(F32), MASKED)

        lane = lax.broadcasted_iota(jnp.int32, (sub_len, LANES), 1)
        for r in range(n_res):
            q2 = q_ref[0, r, 0]
            rows = slice(r * sub_len, (r + 1) * sub_len)
            for hh in range(HEADS_PER_TILE):
                in_half = (lane >= hh * HEAD_DIM) & (lane < (hh + 1) * HEAD_DIM)
                qm_ref[g, hh, rows, :] = jnp.where(in_half, q2, jnp.zeros_like(q2))
            vt_ref[g, :LANES, rows] = v_ref[0, r, 0].astype(F32).T.astype(BF16)
        vt_ref[g, LANES:, :] = jnp.ones((ONES_ROWS, n_res * sub_len), BF16)
        units += [(g, r, n) for r in range(n_res) for n in range(sub_len // blk)]

    def rows_of(unit):
        g, r, n = unit
        sub_len = qkv_refs[g][0].shape[3]
        q_rows = slice(n * blk, (n + 1) * blk)
        k_rows = slice((n - 1) * blk, (n + 1) * blk) if n else q_rows
        shift = lambda sl: slice(r * sub_len + sl.start, r * sub_len + sl.stop)
        return q_rows, k_rows, shift(q_rows), shift(k_rows)

    def scores(unit):
        g, r, n = unit
        _, k_rows, q_flat, _ = rows_of(unit)
        queries = jnp.concatenate([qm_ref[g, 0, q_flat, :], qm_ref[g, 1, q_flat, :]], axis=0)
        s = lax.dot_general(qkv_refs[g][1][0, r, 0, k_rows, :], queries, _NT, preferred_element_type=F32)
        s = s + (bias_ref[g] if n else bias_ref[g, blk:, :])
        return s, jnp.max(s, axis=0, keepdims=True)

    def probs(unit, s, m):
        return jnp.exp2(s - m).astype(BF16), m

    def outputs(unit, p, m):
        g = unit[0]
        _, _, q_flat, k_flat = rows_of(unit)
        o = jnp.dot(vt_ref[g, :, k_flat], p, preferred_element_type=F32)
        l = o[LANES:LANES + 1, :]
        inv = 1.0 / l
        lse = m + jnp.log2(l)
        for hh in range(HEADS_PER_TILE):
            feat = slice(hh * HEAD_DIM, (hh + 1) * HEAD_DIM)
            qcol = slice(hh * blk, (hh + 1) * blk)
            ot_ref[g, feat, q_flat] = o[feat, qcol] * inv[:, qcol]
            lt_ref[g, feat, q_flat] = jnp.broadcast_to(lse[:, qcol], (HEAD_DIM, blk))

    _emit_pipelined(units, (scores, probs, outputs), lag=2)

    for g, (_, dilation) in enumerate(DIL_GROUPS):
        o_ref, lse_ref = out_refs[g]
        n_res, sub_len = qkv_refs[g][0].shape[1], qkv_refs[g][0].shape[3]
        for r in range(n_res):
            rows = slice(r * sub_len, (r + 1) * sub_len)
            if dilation == 1:
                o_ref[0, 0] = ot_ref[g, :, rows].T.astype(o_ref.dtype)
                lse_ref[0, 0] = lt_ref[g, :, rows].T
            else:
                nat_ref[pl.ds(r, sub_len, stride=dilation), :] = ot_ref[g, :, rows].T
                lse_ref.at[0, 0][pl.ds(r, sub_len, stride=dilation), :] = lt_ref[g, :, rows].T
        if dilation > 1:
            o_ref[0, 0] = nat_ref[...].astype(o_ref.dtype)


def _dilated_attention(slopes, qkv_groups):
    batch = qkv_groups[0].shape[0]
    seq = qkv_groups[0].shape[1] * qkv_groups[0].shape[3]
    tiles_per_part = DIL_OUT // LANES
    n_groups = len(qkv_groups)

    def spec(arr, part):
        _, n_res, _, sub_len, _ = arr.shape
        return pl.BlockSpec((1, n_res, 1, sub_len, LANES), lambda b, jp: (b, 0, part * tiles_per_part + jp, 0, 0))

    nat_spec = pl.BlockSpec((1, 1, seq, LANES), lambda b, jp: (b, jp, 0, 0))
    out_shape, out_specs = [], []
    for arr in qkv_groups:
        for dtype in (BF16, F32):
            out_shape.append(jax.ShapeDtypeStruct((batch, tiles_per_part, seq, LANES), dtype))
            out_specs.append(nat_spec)
    outs = pl.pallas_call(
        _dil_kernel,
        grid=(batch, tiles_per_part),
        in_specs=[pl.BlockSpec(memory_space=pltpu.SMEM)] + [spec(arr, part) for arr in qkv_groups for part in range(3)],
        out_specs=out_specs,
        out_shape=out_shape,
        scratch_shapes=[pltpu.VMEM((n_groups, HEADS_PER_TILE, seq, LANES), BF16),
                        pltpu.VMEM((n_groups, LANES + ONES_ROWS, seq), BF16),
                        pltpu.VMEM((n_groups, 2 * DIL_STEPS, 2 * DIL_STEPS), F32),
                        pltpu.VMEM((n_groups, LANES, seq), F32),
                        pltpu.VMEM((n_groups, LANES, seq), F32),
                        pltpu.VMEM((seq, LANES), F32)],
        compiler_params=pltpu.CompilerParams(dimension_semantics=("parallel", "parallel"),
                                             vmem_limit_bytes=VMEM_LIMIT),
        name="dilated_attn",
    )(slopes, *[arr for arr in qkv_groups for _ in range(3)])
    return outs[0::2], outs[1::2]


N_MOBA_BLOCKS = 8


def _moba_prepare(jp, slopes_ref, qkv_ref, qa_ref, ka_ref, vt_ref):
    seq = qkv_ref.shape[2]
    n_pairs = MOBA_WIDTH // LANES
    nb = N_MOBA_BLOCKS
    shift = MOBA_BLOCK.bit_length() - 1
    sel_lanes = HEADS_PER_TILE * nb

    q2 = qkv_ref[0, jp]
    k2 = qkv_ref[0, n_pairs + jp]
    v_t = qkv_ref[0, 2 * n_pairs + jp].astype(F32).T.astype(BF16)
    for hh in range(HEADS_PER_TILE):
        vt_ref[jp, hh, :HEAD_DIM, :] = v_t[hh * HEAD_DIM:(hh + 1) * HEAD_DIM, :]
        vt_ref[jp, hh, HEAD_DIM:, :] = jnp.ones((ONES_ROWS, seq), BF16)

    lane = lax.broadcasted_iota(jnp.int32, (seq, LANES), 1)
    row = lax.broadcasted_iota(jnp.int32, (seq, LANES), 0)
    one_hot = jnp.where((lane & (nb - 1)) == (row >> shift), 1.0, 0.0)
    in_block = (row & (MOBA_BLOCK - 1)).astype(F32)
    n_one_hot = sel_lanes + 3 * nb
    k_extra = jnp.where(lane < n_one_hot, one_hot, jnp.where(lane < n_one_hot + 3, in_block, 0.0))
    ka_ref[jp, :, :LANES] = k2
    ka_ref[jp, :, LANES:] = k_extra.astype(BF16)

    a_row = lax.broadcasted_iota(jnp.int32, (LANES, seq), 0)
    a_col = lax.broadcasted_iota(jnp.int32, (LANES, seq), 1)
    avg = jnp.where((a_row < sel_lanes) & ((a_row & (nb - 1)) == (a_col >> shift)),
                    1.0 / MOBA_BLOCK, 0.0).astype(BF16)
    km = jnp.dot(avg, k2, preferred_element_type=F32)
    km_row = lax.broadcasted_iota(jnp.int32, (LANES, LANES), 0)
    km_lane = lax.broadcasted_iota(jnp.int32, (LANES, LANES), 1)
    km = jnp.where((km_row < sel_lanes) & ((km_row >> 3) == (km_lane >> 6)), km, 0.0)
    gate_t = sum(lax.dot_general(part, q2, _NT, preferred_element_type=F32) for part in _split3(km))

    blk_idx = lax.broadcasted_iota(jnp.int32, (nb, seq), 0)
    own_blk = lax.broadcasted_iota(jnp.int32, (nb, seq), 1) >> shift
    is_past = blk_idx < own_blk
    sels = []
    for hh in range(HEADS_PER_TILE):
        g = jnp.where(is_past, gate_t[hh * nb:(hh + 1) * nb, :], -jnp.inf)
        cnt = jnp.zeros((nb, seq), F32)
        for m in range(nb):
            tie = jnp.where(blk_idx > m, 1.0, 0.0)
            cnt = cnt + jnp.where(g[m:m + 1, :] > g, 1.0, jnp.where(g[m:m + 1, :] == g, tie, 0.0))
        keep = (is_past & (cnt < MOBA_TOPK)) | (blk_idx == own_blk)
        sels.append(jnp.where(keep, 0.0, MASKED))
    sel_t = jnp.concatenate(sels + [jnp.zeros((LANES - sel_lanes, seq), F32)], axis=0)

    q_t = q2.astype(F32).T
    feat = lax.broadcasted_iota(jnp.int32, (LANES, seq), 0)
    lane1 = lax.broadcasted_iota(jnp.int32, (LANES, 1), 0)
    for hh in range(HEADS_PER_TILE):
        slope = slopes_ref[DIL_HEADS + HEADS_PER_TILE * jp + hh] * LOG2E
        base = jnp.where(lane1 < n_one_hot, slope * float(MOBA_BLOCK) * (lane1 & (nb - 1)).astype(F32), slope)
        term = jnp.where(lane1 < n_one_hot, ((lane1 - sel_lanes) >> 3), lane1 - n_one_hot)
        term = jnp.where(lane1 < sel_lanes, -1, term)
        hi, mid, lo = _split3(base)
        q_const = jnp.where(term == 0, hi, jnp.where(term == 1, mid, jnp.where(term == 2, lo, jnp.zeros_like(hi))))
        own_sel = (feat >= hh * nb) & (feat < (hh + 1) * nb)
        in_head = (feat >= hh * HEAD_DIM) & (feat < (hh + 1) * HEAD_DIM)
        qa_ref[jp, hh, :LANES, :] = jnp.where(in_head, q_t, 0.0).astype(BF16)
        qa_ref[jp, hh, LANES:, :] = jnp.where(own_sel, sel_t.astype(BF16), q_const)


def _moba_kernel(slopes_ref, qkv_ref, *refs, n_casts):
    cast_in, (o_ref, *cast_out) = refs[:n_casts], refs[n_casts:2 * n_casts + 1]
    qa_ref, ka_ref, vt_ref, ot_ref = refs[2 * n_casts + 1:]
    for src, dst in zip(cast_in, cast_out):
        dst[...] = src[...].astype(dst.dtype)
    n_pairs = MOBA_WIDTH // LANES
    nb = N_MOBA_BLOCKS
    for jp in range(n_pairs):
        _moba_prepare(jp, slopes_ref, qkv_ref, qa_ref, ka_ref, vt_ref)

    ki = lax.broadcasted_iota(jnp.int32, (MOBA_BLOCK, HEADS_PER_TILE * MOBA_BLOCK), 0)
    qi = lax.broadcasted_iota(jnp.int32, (MOBA_BLOCK, HEADS_PER_TILE * MOBA_BLOCK), 1) & (MOBA_BLOCK - 1)
    causal = jnp.where(ki <= qi, 0.0, MASKED)

    def scores(unit):
        jp, c = unit
        start, stop = c * MOBA_BLOCK, (c + 1) * MOBA_BLOCK
        queries = jnp.concatenate([qa_ref[jp, 0, :, start:stop], qa_ref[jp, 1, :, start:stop]], axis=1)
        s = jnp.dot(ka_ref[jp, :stop, :], queries, preferred_element_type=F32)
        s_own = s[start:, :] + causal
        m = jnp.max(s_own, axis=0, keepdims=True)
        if c:
            m = jnp.maximum(m, jnp.max(s[:start, :], axis=0, keepdims=True))
        return (s[:start, :] if c else None), s_own, m

    def probs(unit, s_past, s_own, m):
        p_own = jnp.exp2(s_own - m).astype(BF16)
        if not unit[1]:
            return (p_own,)
        return (jnp.concatenate([jnp.exp2(s_past - m).astype(BF16), p_own], axis=0),)

    def outputs(unit, p):
        jp, c = unit
        start, stop = c * MOBA_BLOCK, (c + 1) * MOBA_BLOCK
        for hh in range(HEADS_PER_TILE):
            o = jnp.dot(vt_ref[jp, hh, :, :stop], p[:, hh * MOBA_BLOCK:(hh + 1) * MOBA_BLOCK],
                        preferred_element_type=F32)
            ot_ref[jp, hh * HEAD_DIM:(hh + 1) * HEAD_DIM, start:stop] = o[:HEAD_DIM] * (1.0 / o[HEAD_DIM:HEAD_DIM + 1])

    _emit_pipelined([(jp, c) for jp in range(n_pairs) for c in range(nb)], (scores, probs, outputs), lag=1)
    for jp in range(n_pairs):
        o_ref[0, jp] = ot_ref[jp].T.astype(o_ref.dtype)


def _moba_attention(slopes, qkv, layer, stacked_weights):
    batch, n_tiles, seq, _ = qkv.shape
    assert seq == N_MOBA_BLOCKS * MOBA_BLOCK
    n_pairs = MOBA_WIDTH // LANES
    cast_in, cast_out, cast_shape = [], [], []
    for w in stacked_weights:
        _, rows, cols = w.shape
        slab = rows // batch
        assert slab * batch == rows and slab % 16 == 0
        cast_in.append(pl.BlockSpec((None, slab, cols), lambda b: (layer, b, 0)))
        cast_out.append(pl.BlockSpec((slab, cols), lambda b: (b, 0)))
        cast_shape.append(jax.ShapeDtypeStruct((rows, cols), BF16))
    return pl.pallas_call(
        functools.partial(_moba_kernel, n_casts=len(stacked_weights)),
        grid=(batch,),
        in_specs=[pl.BlockSpec(memory_space=pltpu.SMEM),
                  pl.BlockSpec((1, n_tiles, seq, LANES), lambda b: (b, 0, 0, 0))] + cast_in,
        out_specs=[pl.BlockSpec((1, n_pairs, seq, LANES), lambda b: (b, 0, 0, 0))] + cast_out,
        out_shape=[jax.ShapeDtypeStruct((batch, n_pairs, seq, LANES), BF16)] + cast_shape,
        scratch_shapes=[pltpu.VMEM((n_pairs, HEADS_PER_TILE, 2 * LANES, seq), BF16),
                        pltpu.VMEM((n_pairs, seq, 2 * LANES), BF16),
                        pltpu.VMEM((n_pairs, HEADS_PER_TILE, HEAD_DIM + ONES_ROWS, seq), BF16),
                        pltpu.VMEM((n_pairs, LANES, seq), F32)],
        compiler_params=pltpu.CompilerParams(dimension_semantics=("parallel",),
                                             vmem_limit_bytes=VMEM_LIMIT),
        name="moba_attn",
    )(slopes, qkv, *stacked_weights)


MIX_SUB_ROWS = 256


def _mix_out_kernel(x_ref, o0_ref, o1_ref, o2_ref, l0_ref, l1_ref, l2_ref, ob_ref, ga_ref, gb_ref,
                    wa_ref, wb_ref, wo_ref, g_ref, out_ref):
    tm = x_ref.shape[0]
    l_refs, o_refs = (l0_ref, l1_ref, l2_ref), (o0_ref, o1_ref, o2_ref)

    def rows(j):
        return slice(j * MIX_SUB_ROWS, (j + 1) * MIX_SUB_ROWS)

    def load(ref, j):
        return jnp.concatenate([ref[0, t, rows(j), :].astype(F32) for t in range(ref.shape[1])], axis=1)

    def merge(j):
        lses = [load(ref, j) for ref in l_refs]
        outs = [load(ref, j) for ref in o_refs]
        top = jnp.maximum(jnp.maximum(lses[0], lses[1]), lses[2])
        es = [jnp.exp2(l - top) for l in lses]
        inv = 1.0 / (es[0] + es[1] + es[2])
        o_a = (es[0] * inv) * outs[0] + (es[1] * inv) * outs[1] + (es[2] * inv) * outs[2]
        return (o_a.astype(BF16),)

    def branches(j, o_a):
        y_a = jnp.dot(o_a, wa_ref[...], preferred_element_type=F32)
        o_b = jnp.concatenate([ob_ref[0, t, rows(j), :] for t in range(ob_ref.shape[1])], axis=1)
        y_b = jnp.dot(o_b, wb_ref[...], preferred_element_type=F32)
        h_a, h_b = 0.5 * y_a, 0.5 * y_b
        t_a = jnp.tanh(0.5 * ga_ref[rows(j), :].astype(F32))
        t_b = jnp.tanh(0.5 * gb_ref[rows(j), :].astype(F32))
        return (((h_a + h_a * t_a) + (h_b + h_b * t_b)).astype(BF16),)

    def project(j, merged):
        return (jnp.dot(merged, wo_ref[...], preferred_element_type=F32),)

    def finish(j, z):
        out_ref[rows(j), :] = x_ref[rows(j), :] + _rms_norm(z, g_ref[...])

    _emit_pipelined(list(range(tm // MIX_SUB_ROWS)), (merge, branches, project, finish), lag=1)


def _mix_out(x, dil_outs, dil_lses, o_moba, gates, w_a, w_b, w_o, g_post, seq, tm):
    n, d = x.shape
    tiles_per_seq = seq // tm
    row = lambda w, col=0: pl.BlockSpec((tm, w), lambda i: (i, col))

    def group_spec(arr):
        return pl.BlockSpec((1, arr.shape[1], tm, LANES), lambda i: (i // tiles_per_seq, 0, i % tiles_per_seq, 0))

    moba_spec = pl.BlockSpec((1, o_moba.shape[1], tm, LANES), lambda i: (i // tiles_per_seq, 0, i % tiles_per_seq, 0))

    return pl.pallas_call(
        _mix_out_kernel,
        grid=(n // tm,),
        in_specs=[row(d)] + [group_spec(a) for a in dil_outs] + [group_spec(a) for a in dil_lses]
                 + [moba_spec, row(d, 0), row(d, 1),
                    _resident(w_a.shape), _resident(w_b.shape), _resident(w_o.shape), _resident((1, d))],
        out_specs=row(d),
        out_shape=jax.ShapeDtypeStruct((n, d), F32),
        compiler_params=pltpu.CompilerParams(dimension_semantics=("parallel",),
                                             vmem_limit_bytes=VMEM_LIMIT),
        name="mix_out",
    )(x, *dil_outs, *dil_lses, o_moba, gates, gates, w_a, w_b, w_o, g_post)


CARRY_ROWS = 8


def _ffn_kernel(x_ref, gpre_ref, gpost_ref, wg_ref, wu_ref, cw_ref, cb_ref, wd_ref, out_ref, a_ref,
                *, tiles_per_seq):
    tm = x_ref.shape[0]
    x = x_ref[...]
    h = _rms_norm(x, gpre_ref[...]).astype(BF16)

    @pl.when(pl.program_id(0) % tiles_per_seq == 0)
    def _():
        a_ref[0:CARRY_ROWS, :] = jnp.zeros((CARRY_ROWS, a_ref.shape[1]), F32)

    a_ref[CARRY_ROWS:CARRY_ROWS + tm, :] = jnp.dot(h, wg_ref[...], preferred_element_type=F32)
    up = jnp.dot(h, wu_ref[...], preferred_element_type=F32)
    conv = cb_ref[...]
    for tap in range(CONV_WIDTH):
        back = CONV_WIDTH - 1 - tap
        conv = conv + a_ref[CARRY_ROWS - back:CARRY_ROWS - back + tm, :] * cw_ref[tap:tap + 1, :]
    a_ref[0:CARRY_ROWS, :] = a_ref[tm:tm + CARRY_ROWS, :]
    inner = 0.7978845608028654 * (conv + 0.044715 * (conv * conv * conv))
    u = (0.5 * conv * (1.0 + jnp.tanh(inner))) * up
    z = jnp.dot(u.astype(BF16), wd_ref[...], preferred_element_type=F32)
    out_ref[...] = x + _rms_norm(z, gpost_ref[...])


def _ffn(x, g_pre, g_post, w_gate, w_up, conv_w, conv_b, w_down, seq, tm):
    n, d = x.shape
    d_ff = w_gate.shape[1]
    row = pl.BlockSpec((tm, d), lambda i: (i, 0))
    return pl.pallas_call(
        functools.partial(_ffn_kernel, tiles_per_seq=seq // tm),
        grid=(n // tm,),
        in_specs=[row, _resident((1, d)), _resident((1, d)), _resident(w_gate.shape),
                  _resident(w_up.shape), _resident(conv_w.shape), _resident((1, d_ff)),
                  _resident(w_down.shape)],
        out_specs=row,
        out_shape=jax.ShapeDtypeStruct((n, d), F32),
        scratch_shapes=[pltpu.VMEM((tm + CARRY_ROWS, d_ff), F32)],
        compiler_params=pltpu.CompilerParams(dimension_semantics=("arbitrary",),
                                             vmem_limit_bytes=VMEM_LIMIT),
        name="conv_ffn",
    )(x, g_pre, g_post, w_gate, w_up, conv_w, conv_b, w_down)


def kernel(x, mix_norm_pre, mix_norm_post, w_in, w_branch_dil, w_branch_moba, w_out, ffn_norm_pre, ffn_norm_post, w_ffn_gate, w_ffn_up, ffn_conv_w, ffn_conv_b, w_ffn_down):
    batch, seq, d = x.shape
    depth = w_in.shape[0]
    idx = jnp.arange(1, N_ATTN_HEADS + 1, dtype=F32)
    slopes = jnp.exp2(-8.0 * idx / N_ATTN_HEADS)
    qkv_dil_cols = 3 * DIL_WIDTH
    col = jnp.arange(w_in.shape[2])
    is_q = (col < DIL_WIDTH) | ((col >= qkv_dil_cols) & (col < qkv_dil_cols + MOBA_WIDTH))
    q_scale = jnp.where(is_q, SCALE * LOG2E, 1.0).astype(F32)

    xf = x.reshape(batch * seq, d)
    for l in range(depth):
        w = (w_in[l] * q_scale).astype(BF16)
        *qkv_groups, qkv_moba, gates = _inproj(xf, mix_norm_pre[l][None], w, batch, seq, MIXER_TILE_ROWS)
        dil_outs, dil_lses = _dilated_attention(slopes, qkv_groups)
        o_moba, w_a, w_b, w_o, w_gate, w_up, w_down = _moba_attention(
            slopes, qkv_moba, l, (w_branch_dil, w_branch_moba, w_out, w_ffn_gate, w_ffn_up, w_ffn_down))
        xf = _mix_out(xf, dil_outs, dil_lses, o_moba, gates, w_a, w_b, w_o,
                      mix_norm_post[l][None], seq, MIXER_TILE_ROWS)
        xf = _ffn(xf, ffn_norm_pre[l][None], ffn_norm_post[l][None], w_gate, w_up,
                  ffn_conv_w[l], ffn_conv_b[l][None], w_down, seq, FFN_TILE_ROWS)
    return xf.reshape(batch, seq, d)
```

```python
import functools

import jax
import jax.numpy as jnp
from jax import lax
from jax.experimental import pallas as pl
from jax.experimental.pallas import tpu as pltpu

F32 = jnp.float32
BF16 = jnp.bfloat16

HEAD_DIM = 64
DIL_GROUPS = ((128, 1), (512, 4), (2048, 16))
DIL_HEADS_PER_GROUP = 4
DIL_HEADS = DIL_HEADS_PER_GROUP * len(DIL_GROUPS)
MOBA_HEADS = 4
MOBA_BLOCK = 256
MOBA_TOPK = 3
N_ATTN_HEADS = DIL_HEADS + MOBA_HEADS
DIL_WIDTH = DIL_HEADS * HEAD_DIM
MOBA_WIDTH = MOBA_HEADS * HEAD_DIM
DIL_OUT = DIL_HEADS_PER_GROUP * HEAD_DIM
DIL_STEPS = 128
RMS_EPS = 1e-6
SCALE = HEAD_DIM ** -0.5
CONV_WIDTH = 3

LANES = 128
HEADS_PER_TILE = LANES // HEAD_DIM
MASKED = -(2.0 ** 100)
VMEM_LIMIT = 56 * 1024 * 1024
MIXER_TILE_ROWS = 1024
FFN_TILE_ROWS = 512

LOG2E = 1.4426950408889634
ONES_ROWS = 16

_NT = (((1,), (1,)), ((), ()))


def _emit_pipelined(units, stages, lag):
    done = [dict() for _ in stages]
    for t in range(len(units) + lag * (len(stages) - 1)):
        for k, stage in enumerate(stages):
            u = t - lag * k
            if 0 <= u < len(units):
                prev = done[k - 1].pop(u) if k else ()
                done[k][u] = stage(units[u], *prev)


def _rms_norm(x, g):
    return x * lax.rsqrt(jnp.mean(x * x, axis=-1, keepdims=True) + RMS_EPS) * g


def _resident(shape):
    return pl.BlockSpec(shape, lambda *_: (0,) * len(shape), pipeline_mode=pl.Buffered(1))


def _split3(x):
    hi = x.astype(BF16)
    rem = x - hi.astype(F32)
    mid = rem.astype(BF16)
    lo = (rem - mid.astype(F32)).astype(BF16)
    return hi, mid, lo


def _inproj_kernel(x_ref, g_ref, w_ref, d0_ref, d1_ref, d2_ref, moba_ref, gate_ref):
    tm = x_ref.shape[0]
    h = _rms_norm(x_ref[...], g_ref[...]).astype(BF16)

    def project(col, width):
        return jnp.dot(h, w_ref[:, col:col + width], preferred_element_type=F32)

    def store_tiles(dst, first_tile, res):
        for t in range(res.shape[-1] // LANES):
            dst[..., first_tile + t, :, :] = res[..., t * LANES:(t + 1) * LANES].astype(BF16)

    tiles_per_part = DIL_OUT // LANES
    for g, (o_ref, (_, dil)) in enumerate(zip((d0_ref, d1_ref, d2_ref), DIL_GROUPS)):
        for part in range(3):
            res = project(part * DIL_WIDTH + g * DIL_OUT, DIL_OUT)
            res = jnp.swapaxes(res.reshape(tm // dil, dil, DIL_OUT), 0, 1) if dil > 1 else res[None]
            store_tiles(o_ref.at[0], part * tiles_per_part, res)
    moba0 = 3 * DIL_WIDTH
    store_tiles(moba_ref.at[0], 0, project(moba0, 3 * MOBA_WIDTH))
    gate0 = moba0 + 3 * MOBA_WIDTH
    chunk = 512
    for c in range(0, gate_ref.shape[1], chunk):
        gate_ref[:, c:c + chunk] = project(gate0 + c, chunk).astype(BF16)


def _inproj(x, g, w, batch, seq, tm):
    n, d = x.shape
    tiles_per_seq = seq // tm
    group_w = 3 * DIL_OUT
    moba_w = 3 * MOBA_WIDTH
    gate_w = w.shape[1] - 3 * DIL_WIDTH - moba_w

    group_tiles, moba_tiles = group_w // LANES, moba_w // LANES
    tile_map = lambda i: (i // tiles_per_seq, 0, 0, i % tiles_per_seq, 0)
    dils = [dil for _, dil in DIL_GROUPS]
    return pl.pallas_call(
        _inproj_kernel,
        grid=(n // tm,),
        in_specs=[pl.BlockSpec((tm, d), lambda i: (i, 0)), _resident((1, d)), _resident(w.shape)],
        out_specs=[pl.BlockSpec((1, dil, group_tiles, tm // dil, LANES), tile_map) for dil in dils]
                  + [pl.BlockSpec((1, moba_tiles, tm, LANES), lambda i: (i // tiles_per_seq, 0, i % tiles_per_seq, 0)),
                     pl.BlockSpec((tm, gate_w), lambda i: (i, 0))],
        out_shape=[jax.ShapeDtypeStruct((batch, dil, group_tiles, seq // dil, LANES), BF16) for dil in dils]
                  + [jax.ShapeDtypeStruct((batch, moba_tiles, seq, LANES), BF16),
                     jax.ShapeDtypeStruct((n, gate_w), BF16)],
        compiler_params=pltpu.CompilerParams(dimension_semantics=("parallel",),
                                             vmem_limit_bytes=VMEM_LIMIT),
        name="inproj",
    )(x, g, w)


def _dil_kernel(slopes_ref, *refs):
    n_groups = len(DIL_GROUPS)
    qkv_refs = [refs[3 * g:3 * g + 3] for g in range(n_groups)]
    out_refs = [refs[3 * n_groups + 2 * g:3 * n_groups + 2 * g + 2] for g in range(n_groups)]
    qm_ref, vt_ref, bias_ref, ot_ref, lt_ref, nat_ref = refs[5 * n_groups:]
    jp = pl.program_id(1)
    blk = DIL_STEPS

    kk = lax.broadcasted_iota(jnp.int32, (2 * blk, 2 * blk), 0)
    col = lax.broadcasted_iota(jnp.int32, (2 * blk, 2 * blk), 1)
    delta = blk + (col & (blk - 1)) - kk
    window = (delta >= 0) & (delta <= DIL_STEPS)

    units = []
    for g, (_, dilation) in enumerate(DIL_GROUPS):
        q_ref, k_ref, v_ref = qkv_refs[g]
        n_res, sub_len = q_ref.shape[1], q_ref.shape[3]
        head0 = g * DIL_HEADS_PER_GROUP
        slope = jnp.where(col < blk, slopes_ref[head0 + HEADS_PER_TILE * jp],
                          slopes_ref[head0 + HEADS_PER_TILE * jp + 1]) * (float(dilation) * LOG2E)
        bias_ref[g] = jnp.where(window, -slope * delta.astype(F32), MASKED)

        lane = lax.broadcasted_iota(jnp.int32, (sub_len, LANES), 1)
        for r in range(n_res):
            q2 = q_ref[0, r, 0]
            rows = slice(r * sub_len, (r + 1) * sub_len)
            for hh in range(HEADS_PER_TILE):
                in_half = (lane >= hh * HEAD_DIM) & (lane < (hh + 1) * HEAD_DIM)
                qm_ref[g, hh, rows, :] = jnp.where(in_half, q2, jnp.zeros_like(q2))
            vt_ref[g, :LANES, rows] = v_ref[0, r, 0].astype(F32).T.astype(BF16)
        vt_ref[g, LANES:, :] = jnp.ones((ONES_ROWS, n_res * sub_len), BF16)
        units += [(g, r, n) for r in range(n_res) for n in range(sub_len // blk)]

    def rows_of(unit):
        g, r, n = unit
        sub_len = qkv_refs[g][0].shape[3]
        q_rows = slice(n * blk, (n + 1) * blk)
        k_rows = slice((n - 1) * blk, (n + 1) * blk) if n else q_rows
        shift = lambda sl: slice(r * sub_len + sl.start, r * sub_len + sl.stop)
        return q_rows, k_rows, shift(q_rows), shift(k_rows)

    def scores(unit):
        g, r, n = unit
        _, k_rows, q_flat, _ = rows_of(unit)
        queries = jnp.concatenate([qm_ref[g, 0, q_flat, :], qm_ref[g, 1, q_flat, :]], axis=0)
        s = lax.dot_general(qkv_refs[g][1][0, r, 0, k_rows, :], queries, _NT, preferred_element_type=F32)
        s = s + (bias_ref[g] if n else bias_ref[g, blk:, :])
        return s, jnp.max(s, axis=0, keepdims=True)

    def probs(unit, s, m):
        return jnp.exp2(s - m).astype(BF16), m

    def outputs(unit, p, m):
        g = unit[0]
        _, _, q_flat, k_flat = rows_of(unit)
        o = jnp.dot(vt_ref[g, :, k_flat], p, preferred_element_type=F32)
        l = o[LANES:LANES + 1, :]
        inv = 1.0 / l
        lse = m + jnp.log2(l)
        for hh in range(HEADS_PER_TILE):
            feat = slice(hh * HEAD_DIM, (hh + 1) * HEAD_DIM)
            qcol = slice(hh * blk, (hh + 1) * blk)
            ot_ref[g, feat, q_flat] = o[feat, qcol] * inv[:, qcol]
            lt_ref[g, feat, q_flat] = jnp.broadcast_to(lse[:, qcol], (HEAD_DIM, blk))

    _emit_pipelined(units, (scores, probs, outputs), lag=2)

    for g, (_, dilation) in enumerate(DIL_GROUPS):
        o_ref, lse_ref = out_refs[g]
        n_res, sub_len = qkv_refs[g][0].shape[1], qkv_refs[g][0].shape[3]
        for r in range(n_res):
            rows = slice(r * sub_len, (r + 1) * sub_len)
            if dilation == 1:
                o_ref[0, 0] = ot_ref[g, :, rows].T.astype(o_ref.dtype)
                lse_ref[0, 0] = lt_ref[g, :, rows].T
            else:
                nat_ref[pl.ds(r, sub_len, stride=dilation), :] = ot_ref[g, :, rows].T
                lse_ref.at[0, 0][pl.ds(r, sub_len, stride=dilation), :] = lt_ref[g, :, rows].T
        if dilation > 1:
            o_ref[0, 0] = nat_ref[...].astype(o_ref.dtype)


def _dilated_attention(slopes, qkv_groups):
    batch = qkv_groups[0].shape[0]
    seq = qkv_groups[0].shape[1] * qkv_groups[0].shape[3]
    tiles_per_part = DIL_OUT // LANES
    n_groups = len(qkv_groups)

    def spec(arr, part):
        _, n_res, _, sub_len, _ = arr.shape
        return pl.BlockSpec((1, n_res, 1, sub_len, LANES), lambda b, jp: (b, 0, part * tiles_per_part + jp, 0, 0))

    nat_spec = pl.BlockSpec((1, 1, seq, LANES), lambda b, jp: (b, jp, 0, 0))
    out_shape, out_specs = [], []
    for arr in qkv_groups:
        for dtype in (BF16, F32):
            out_shape.append(jax.ShapeDtypeStruct((batch, tiles_per_part, seq, LANES), dtype))
            out_specs.append(nat_spec)
    outs = pl.pallas_call(
        _dil_kernel,
        grid=(batch, tiles_per_part),
        in_specs=[pl.BlockSpec(memory_space=pltpu.SMEM)] + [spec(arr, part) for arr in qkv_groups for part in range(3)],
        out_specs=out_specs,
        out_shape=out_shape,
        scratch_shapes=[pltpu.VMEM((n_groups, HEADS_PER_TILE, seq, LANES), BF16),
                        pltpu.VMEM((n_groups, LANES + ONES_ROWS, seq), BF16),
                        pltpu.VMEM((n_groups, 2 * DIL_STEPS, 2 * DIL_STEPS), F32),
                        pltpu.VMEM((n_groups, LANES, seq), F32),
                        pltpu.VMEM((n_groups, LANES, seq), F32),
                        pltpu.VMEM((seq, LANES), F32)],
        compiler_params=pltpu.CompilerParams(dimension_semantics=("parallel", "parallel"),
                                             vmem_limit_bytes=VMEM_LIMIT),
        name="dilated_attn",
    )(slopes, *[arr for arr in qkv_groups for _ in range(3)])
    return outs[0::2], outs[1::2]


N_MOBA_BLOCKS = 8


def _moba_prepare(jp, slopes_ref, qkv_ref, qa_ref, ka_ref, vt_ref):
    seq = qkv_ref.shape[2]
    n_pairs = MOBA_WIDTH // LANES
    nb = N_MOBA_BLOCKS
    shift = MOBA_BLOCK.bit_length() - 1
    sel_lanes = HEADS_PER_TILE * nb

    q2 = qkv_ref[0, jp]
    k2 = qkv_ref[0, n_pairs + jp]
    v_t = qkv_ref[0, 2 * n_pairs + jp].astype(F32).T.astype(BF16)
    for hh in range(HEADS_PER_TILE):
        vt_ref[jp, hh, :HEAD_DIM, :] = v_t[hh * HEAD_DIM:(hh + 1) * HEAD_DIM, :]
        vt_ref[jp, hh, HEAD_DIM:, :] = jnp.ones((ONES_ROWS, seq), BF16)

    lane = lax.broadcasted_iota(jnp.int32, (seq, LANES), 1)
    row = lax.broadcasted_iota(jnp.int32, (seq, LANES), 0)
    one_hot = jnp.where((lane & (nb - 1)) == (row >> shift), 1.0, 0.0)
    in_block = (row & (MOBA_BLOCK - 1)).astype(F32)
    n_one_hot = sel_lanes + 3 * nb
    k_extra = jnp.where(lane < n_one_hot, one_hot, jnp.where(lane < n_one_hot + 3, in_block, 0.0))
    ka_ref[jp, :, :LANES] = k2
    ka_ref[jp, :, LANES:] = k_extra.astype(BF16)

    a_row = lax.broadcasted_iota(jnp.int32, (LANES, seq), 0)
    a_col = lax.broadcasted_iota(jnp.int32, (LANES, seq), 1)
    avg = jnp.where((a_row < sel_lanes) & ((a_row & (nb - 1)) == (a_col >> shift)),
                    1.0 / MOBA_BLOCK, 0.0).astype(BF16)
    km = jnp.dot(avg, k2, preferred_element_type=F32)
    km_row = lax.broadcasted_iota(jnp.int32, (LANES, LANES), 0)
    km_lane = lax.broadcasted_iota(jnp.int32, (LANES, LANES), 1)
    km = jnp.where((km_row < sel_lanes) & ((km_row // nb) == (km_lane // HEAD_DIM)), km, 0.0)
    gate_t = sum(lax.dot_general(part, q2, _NT, preferred_element_type=F32) for part in _split3(km))

    blk_idx = lax.broadcasted_iota(jnp.int32, (nb, seq), 0)
    own_blk = lax.broadcasted_iota(jnp.int32, (nb, seq), 1) >> shift
    is_past = blk_idx < own_blk
    sels = []
    for hh in range(HEADS_PER_TILE):
        g = jnp.where(is_past, gate_t[hh * nb:(hh + 1) * nb, :], -jnp.inf)
        cnt = jnp.zeros((nb, seq), F32)
        for m in range(nb):
            tie = jnp.where(blk_idx > m, 1.0, 0.0)
            cnt = cnt + jnp.where(g[m:m + 1, :] > g, 1.0, jnp.where(g[m:m + 1, :] == g, tie, 0.0))
        keep = (is_past & (cnt < MOBA_TOPK)) | (blk_idx == own_blk)
        sels.append(jnp.where(keep, 0.0, MASKED))
    sel_t = jnp.concatenate(sels + [jnp.zeros((LANES - sel_lanes, seq), F32)], axis=0)

    q_t = q2.astype(F32).T
    feat = lax.broadcasted_iota(jnp.int32, (LANES, seq), 0)
    lane1 = lax.broadcasted_iota(jnp.int32, (LANES, 1), 0)
    for hh in range(HEADS_PER_TILE):
        slope = slopes_ref[DIL_HEADS + HEADS_PER_TILE * jp + hh] * LOG2E
        base = jnp.where(lane1 < n_one_hot, slope * float(MOBA_BLOCK) * (lane1 & (nb - 1)).astype(F32), slope)
        term = jnp.where(lane1 < n_one_hot, (lane1 - sel_lanes) // nb, lane1 - n_one_hot)
        term = jnp.where(lane1 < sel_lanes, -1, term)
        hi, mid, lo = _split3(base)
        q_const = jnp.where(term == 0, hi, jnp.where(term == 1, mid, jnp.where(term == 2, lo, jnp.zeros_like(hi))))
        own_sel = (feat >= hh * nb) & (feat < (hh + 1) * nb)
        in_head = (feat >= hh * HEAD_DIM) & (feat < (hh + 1) * HEAD_DIM)
        qa_ref[jp, hh, :LANES, :] = jnp.where(in_head, q_t, 0.0).astype(BF16)
        qa_ref[jp, hh, LANES:, :] = jnp.where(own_sel, sel_t.astype(BF16), q_const)


def _moba_kernel(slopes_ref, qkv_ref, *refs, n_casts, scaled_cast):
    n_in = n_casts + (2 if scaled_cast else 0)
    n_out = 1 + n_casts + (1 if scaled_cast else 0)
    ins, outs = refs[:n_in], refs[n_in:n_in + n_out]
    qa_ref, ka_ref, vt_ref, ot_ref = refs[n_in + n_out:]
    o_ref = outs[0]
    for src, dst in zip(ins[:n_casts], outs[1:1 + n_casts]):
        dst[...] = src[...].astype(dst.dtype)
    if scaled_cast:
        w_ref, col_scale_ref = ins[n_casts:]
        outs[-1][...] = (w_ref[...] * col_scale_ref[...]).astype(outs[-1].dtype)
    n_pairs = MOBA_WIDTH // LANES
    nb = N_MOBA_BLOCKS
    for jp in range(n_pairs):
        _moba_prepare(jp, slopes_ref, qkv_ref, qa_ref, ka_ref, vt_ref)

    ki = lax.broadcasted_iota(jnp.int32, (MOBA_BLOCK, HEADS_PER_TILE * MOBA_BLOCK), 0)
    qi = lax.broadcasted_iota(jnp.int32, (MOBA_BLOCK, HEADS_PER_TILE * MOBA_BLOCK), 1) & (MOBA_BLOCK - 1)
    causal = jnp.where(ki <= qi, 0.0, MASKED)

    def scores(unit):
        jp, c = unit
        start, stop = c * MOBA_BLOCK, (c + 1) * MOBA_BLOCK
        queries = jnp.concatenate([qa_ref[jp, 0, :, start:stop], qa_ref[jp, 1, :, start:stop]], axis=1)
        s = jnp.dot(ka_ref[jp, :stop, :], queries, preferred_element_type=F32)
        s_own = s[start:, :] + causal
        m = jnp.max(s_own, axis=0, keepdims=True)
        if c:
            m = jnp.maximum(m, jnp.max(s[:start, :], axis=0, keepdims=True))
        return (s[:start, :] if c else None), s_own, m

    def probs(unit, s_past, s_own, m):
        p_own = jnp.exp2(s_own - m).astype(BF16)
        if not unit[1]:
            return (p_own,)
        return (jnp.concatenate([jnp.exp2(s_past - m).astype(BF16), p_own], axis=0),)

    def outputs(unit, p):
        jp, c = unit
        start, stop = c * MOBA_BLOCK, (c + 1) * MOBA_BLOCK
        for hh in range(HEADS_PER_TILE):
            o = jnp.dot(vt_ref[jp, hh, :, :stop], p[:, hh * MOBA_BLOCK:(hh + 1) * MOBA_BLOCK],
                        preferred_element_type=F32)
            ot_ref[jp, hh * HEAD_DIM:(hh + 1) * HEAD_DIM, start:stop] = o[:HEAD_DIM] * (1.0 / o[HEAD_DIM:HEAD_DIM + 1])

    _emit_pipelined([(jp, c) for jp in range(n_pairs) for c in range(nb)], (scores, probs, outputs), lag=1)
    for jp in range(n_pairs):
        o_ref[0, jp] = ot_ref[jp].T.astype(o_ref.dtype)


def _moba_attention(slopes, qkv, layer, stacked_weights, scaled=None):
    batch, n_tiles, seq, _ = qkv.shape
    assert seq == N_MOBA_BLOCKS * MOBA_BLOCK
    n_pairs = MOBA_WIDTH // LANES
    cast_in, cast_out, cast_shape, extra = [], [], [], []
    casts = [(w, layer) for w in stacked_weights] + ([scaled[:2]] if scaled else [])
    for w, index in casts:
        _, rows, cols = w.shape
        slab = rows // batch
        assert slab * batch == rows and slab % 16 == 0
        cast_in.append(pl.BlockSpec((None, slab, cols), lambda b, index=index: (index, b, 0)))
        cast_out.append(pl.BlockSpec((slab, cols), lambda b: (b, 0)))
        cast_shape.append(jax.ShapeDtypeStruct((rows, cols), BF16))
        extra.append(w)
    if scaled:
        cast_in.append(_resident(scaled[2].shape))
        extra.append(scaled[2])
    return pl.pallas_call(
        functools.partial(_moba_kernel, n_casts=len(stacked_weights), scaled_cast=scaled is not None),
        grid=(batch,),
        in_specs=[pl.BlockSpec(memory_space=pltpu.SMEM),
                  pl.BlockSpec((1, n_tiles, seq, LANES), lambda b: (b, 0, 0, 0))] + cast_in,
        out_specs=[pl.BlockSpec((1, n_pairs, seq, LANES), lambda b: (b, 0, 0, 0))] + cast_out,
        out_shape=[jax.ShapeDtypeStruct((batch, n_pairs, seq, LANES), BF16)] + cast_shape,
        scratch_shapes=[pltpu.VMEM((n_pairs, HEADS_PER_TILE, 2 * LANES, seq), BF16),
                        pltpu.VMEM((n_pairs, seq, 2 * LANES), BF16),
                        pltpu.VMEM((n_pairs, HEADS_PER_TILE, HEAD_DIM + ONES_ROWS, seq), BF16),
                        pltpu.VMEM((n_pairs, LANES, seq), F32)],
        compiler_params=pltpu.CompilerParams(dimension_semantics=("parallel",),
                                             vmem_limit_bytes=VMEM_LIMIT),
        name="moba_attn",
    )(slopes, qkv, *extra)


MIX_SUB_ROWS = 256


def _mix_out_kernel(x_ref, o0_ref, o1_ref, o2_ref, l0_ref, l1_ref, l2_ref, ob_ref, ga_ref, gb_ref,
                    wa_ref, wb_ref, wo_ref, g_ref, out_ref):
    tm = x_ref.shape[0]
    l_refs, o_refs = (l0_ref, l1_ref, l2_ref), (o0_ref, o1_ref, o2_ref)

    def rows(j):
        return slice(j * MIX_SUB_ROWS, (j + 1) * MIX_SUB_ROWS)

    def load(ref, j):
        return jnp.concatenate([ref[0, t, rows(j), :].astype(F32) for t in range(ref.shape[1])], axis=1)

    def merge(j):
        lses = [load(ref, j) for ref in l_refs]
        outs = [load(ref, j) for ref in o_refs]
        top = jnp.maximum(jnp.maximum(lses[0], lses[1]), lses[2])
        es = [jnp.exp2(l - top) for l in lses]
        inv = 1.0 / (es[0] + es[1] + es[2])
        o_a = (es[0] * inv) * outs[0] + (es[1] * inv) * outs[1] + (es[2] * inv) * outs[2]
        return (o_a.astype(BF16),)

    def branches(j, o_a):
        y_a = jnp.dot(o_a, wa_ref[...], preferred_element_type=F32)
        o_b = jnp.concatenate([ob_ref[0, t, rows(j), :] for t in range(ob_ref.shape[1])], axis=1)
        y_b = jnp.dot(o_b, wb_ref[...], preferred_element_type=F32)
        h_a, h_b = 0.5 * y_a, 0.5 * y_b
        t_a = jnp.tanh(0.5 * ga_ref[rows(j), :].astype(F32))
        t_b = jnp.tanh(0.5 * gb_ref[rows(j), :].astype(F32))
        return (((h_a + h_a * t_a) + (h_b + h_b * t_b)).astype(BF16),)

    def project(j, merged):
        return (jnp.dot(merged, wo_ref[...], preferred_element_type=F32),)

    def finish(j, z):
        out_ref[rows(j), :] = x_ref[rows(j), :] + _rms_norm(z, g_ref[...])

    _emit_pipelined(list(range(tm // MIX_SUB_ROWS)), (merge, branches, project, finish), lag=1)


def _mix_out(x, dil_outs, dil_lses, o_moba, gates, w_a, w_b, w_o, g_post, seq, tm):
    n, d = x.shape
    tiles_per_seq = seq // tm
    row = lambda w, col=0: pl.BlockSpec((tm, w), lambda i: (i, col))

    def group_spec(arr):
        return pl.BlockSpec((1, arr.shape[1], tm, LANES), lambda i: (i // tiles_per_seq, 0, i % tiles_per_seq, 0))

    moba_spec = pl.BlockSpec((1, o_moba.shape[1], tm, LANES), lambda i: (i // tiles_per_seq, 0, i % tiles_per_seq, 0))

    return pl.pallas_call(
        _mix_out_kernel,
        grid=(n // tm,),
        in_specs=[row(d)] + [group_spec(a) for a in dil_outs] + [group_spec(a) for a in dil_lses]
                 + [moba_spec, row(d, 0), row(d, 1),
                    _resident(w_a.shape), _resident(w_b.shape), _resident(w_o.shape), _resident((1, d))],
        out_specs=row(d),
        out_shape=jax.ShapeDtypeStruct((n, d), F32),
        compiler_params=pltpu.CompilerParams(dimension_semantics=("parallel",),
                                             vmem_limit_bytes=VMEM_LIMIT),
        name="mix_out",
    )(x, *dil_outs, *dil_lses, o_moba, gates, gates, w_a, w_b, w_o, g_post)


CARRY_ROWS = 8


def _ffn_kernel(x_ref, gpre_ref, gpost_ref, wg_ref, wu_ref, cw_ref, cb_ref, wd_ref, out_ref, a_ref,
                *, tiles_per_seq):
    tm = x_ref.shape[0]
    x = x_ref[...]
    h = _rms_norm(x, gpre_ref[...]).astype(BF16)

    @pl.when(pl.program_id(0) % tiles_per_seq == 0)
    def _():
        a_ref[0:CARRY_ROWS, :] = jnp.zeros((CARRY_ROWS, a_ref.shape[1]), F32)

    a_ref[CARRY_ROWS:CARRY_ROWS + tm, :] = jnp.dot(h, wg_ref[...], preferred_element_type=F32)
    up = jnp.dot(h, wu_ref[...], preferred_element_type=F32)
    conv = cb_ref[...]
    for tap in range(CONV_WIDTH):
        back = CONV_WIDTH - 1 - tap
        conv = conv + a_ref[CARRY_ROWS - back:CARRY_ROWS - back + tm, :] * cw_ref[tap:tap + 1, :]
    a_ref[0:CARRY_ROWS, :] = a_ref[tm:tm + CARRY_ROWS, :]
    inner = 0.7978845608028654 * (conv + 0.044715 * (conv * conv * conv))
    u = (0.5 * conv * (1.0 + jnp.tanh(inner))) * up
    z = jnp.dot(u.astype(BF16), wd_ref[...], preferred_element_type=F32)
    out_ref[...] = x + _rms_norm(z, gpost_ref[...])


def _ffn(x, g_pre, g_post, w_gate, w_up, conv_w, conv_b, w_down, seq, tm):
    n, d = x.shape
    d_ff = w_gate.shape[1]
    row = pl.BlockSpec((tm, d), lambda i: (i, 0))
    return pl.pallas_call(
        functools.partial(_ffn_kernel, tiles_per_seq=seq // tm),
        grid=(n // tm,),
        in_specs=[row, _resident((1, d)), _resident((1, d)), _resident(w_gate.shape),
                  _resident(w_up.shape), _resident(conv_w.shape), _resident((1, d_ff)),
                  _resident(w_down.shape)],
        out_specs=row,
        out_shape=jax.ShapeDtypeStruct((n, d), F32),
        scratch_shapes=[pltpu.VMEM((tm + CARRY_ROWS, d_ff), F32)],
        compiler_params=pltpu.CompilerParams(dimension_semantics=("arbitrary",),
                                             vmem_limit_bytes=VMEM_LIMIT),
        name="conv_ffn",
    )(x, g_pre, g_post, w_gate, w_up, conv_w, conv_b, w_down)


def kernel(x, mix_norm_pre, mix_norm_post, w_in, w_branch_dil, w_branch_moba, w_out, ffn_norm_pre, ffn_norm_post, w_ffn_gate, w_ffn_up, ffn_conv_w, ffn_conv_b, w_ffn_down):
    batch, seq, d = x.shape
    depth = w_in.shape[0]
    idx = jnp.arange(1, N_ATTN_HEADS + 1, dtype=F32)
    slopes = jnp.exp2(-8.0 * idx / N_ATTN_HEADS)
    qkv_dil_cols = 3 * DIL_WIDTH
    col = jnp.arange(w_in.shape[2])
    is_q = (col < DIL_WIDTH) | ((col >= qkv_dil_cols) & (col < qkv_dil_cols + MOBA_WIDTH))
    q_scale = jnp.where(is_q, SCALE * LOG2E, 1.0).astype(F32)

    xf = x.reshape(batch * seq, d)
    w = (w_in[0] * q_scale).astype(BF16)
    for l in range(depth):
        *qkv_groups, qkv_moba, gates = _inproj(xf, mix_norm_pre[l][None], w, batch, seq, MIXER_TILE_ROWS)
        dil_outs, dil_lses = _dilated_attention(slopes, qkv_groups)
        next_w_in = (w_in, l + 1, q_scale[None]) if l + 1 < depth else None
        o_moba, w_a, w_b, w_o, w_gate, w_up, w_down, *w_next = _moba_attention(
            slopes, qkv_moba, l, (w_branch_dil, w_branch_moba, w_out, w_ffn_gate, w_ffn_up, w_ffn_down),
            scaled=next_w_in)
        if w_next:
            w = w_next[0]
        xf = _mix_out(xf, dil_outs, dil_lses, o_moba, gates, w_a, w_b, w_o,
                      mix_norm_post[l][None], seq, MIXER_TILE_ROWS)
        xf = _ffn(xf, ffn_norm_pre[l][None], ffn_norm_post[l][None], w_gate, w_up,
                  ffn_conv_w[l], ffn_conv_b[l][None], w_down, seq, FFN_TILE_ROWS)
    return xf.reshape(batch, seq, d)
```

```python
import functools

import jax
import jax.numpy as jnp
from jax import lax
from jax.experimental import pallas as pl
from jax.experimental.pallas import tpu as pltpu

F32 = jnp.float32
BF16 = jnp.bfloat16

HEAD_DIM = 64
DIL_GROUPS = ((128, 1), (512, 4), (2048, 16))
DIL_HEADS_PER_GROUP = 4
DIL_HEADS = DIL_HEADS_PER_GROUP * len(DIL_GROUPS)
MOBA_HEADS = 4
MOBA_BLOCK = 256
MOBA_TOPK = 3
N_ATTN_HEADS = DIL_HEADS + MOBA_HEADS
DIL_WIDTH = DIL_HEADS * HEAD_DIM
MOBA_WIDTH = MOBA_HEADS * HEAD_DIM
DIL_OUT = DIL_HEADS_PER_GROUP * HEAD_DIM
DIL_STEPS = 128
RMS_EPS = 1e-6
SCALE = HEAD_DIM ** -0.5
CONV_WIDTH = 3

LANES = 128
HEADS_PER_TILE = LANES // HEAD_DIM
MASKED = -(2.0 ** 100)
VMEM_LIMIT = 56 * 1024 * 1024
MIXER_TILE_ROWS = 1024
FFN_TILE_ROWS = 512

LOG2E = 1.4426950408889634
ONES_ROWS = 16

_NT = (((1,), (1,)), ((), ()))


def _emit_pipelined(units, stages, lag):
    done = [dict() for _ in stages]
    for t in range(len(units) + lag * (len(stages) - 1)):
        for k, stage in enumerate(stages):
            u = t - lag * k
            if 0 <= u < len(units):
                prev = done[k - 1].pop(u) if k else ()
                done[k][u] = stage(units[u], *prev)


def _rms_norm(x, g):
    return x * lax.rsqrt(jnp.mean(x * x, axis=-1, keepdims=True) + RMS_EPS) * g


def _resident(shape):
    return pl.BlockSpec(shape, lambda *_: (0,) * len(shape), pipeline_mode=pl.Buffered(1))


def _split3(x):
    hi = x.astype(BF16)
    rem = x - hi.astype(F32)
    mid = rem.astype(BF16)
    lo = (rem - mid.astype(F32)).astype(BF16)
    return hi, mid, lo


def _inproj_kernel(x_ref, g_ref, w_ref, d0_ref, d1_ref, d2_ref, moba_ref, gate_ref):
    tm = x_ref.shape[0]
    h = _rms_norm(x_ref[...], g_ref[...]).astype(BF16)

    def project(col, width):
        return jnp.dot(h, w_ref[:, col:col + width], preferred_element_type=F32)

    def store_tiles(dst, first_tile, res):
        for t in range(res.shape[-1] // LANES):
            dst[..., first_tile + t, :, :] = res[..., t * LANES:(t + 1) * LANES].astype(BF16)

    tiles_per_part = DIL_OUT // LANES
    for g, (o_ref, (_, dil)) in enumerate(zip((d0_ref, d1_ref, d2_ref), DIL_GROUPS)):
        for part in range(3):
            res = project(part * DIL_WIDTH + g * DIL_OUT, DIL_OUT)
            res = jnp.swapaxes(res.reshape(tm // dil, dil, DIL_OUT), 0, 1) if dil > 1 else res[None]
            store_tiles(o_ref.at[0], part * tiles_per_part, res)
    moba0 = 3 * DIL_WIDTH
    store_tiles(moba_ref.at[0], 0, project(moba0, 3 * MOBA_WIDTH))
    gate0 = moba0 + 3 * MOBA_WIDTH
    chunk = 512
    for c in range(0, gate_ref.shape[1], chunk):
        gate_ref[:, c:c + chunk] = jnp.tanh(0.5 * project(gate0 + c, chunk)).astype(BF16)


def _inproj(x, g, w, batch, seq, tm):
    n, d = x.shape
    tiles_per_seq = seq // tm
    group_w = 3 * DIL_OUT
    moba_w = 3 * MOBA_WIDTH
    gate_w = w.shape[1] - 3 * DIL_WIDTH - moba_w

    group_tiles, moba_tiles = group_w // LANES, moba_w // LANES
    tile_map = lambda i: (i // tiles_per_seq, 0, 0, i % tiles_per_seq, 0)
    dils = [dil for _, dil in DIL_GROUPS]
    return pl.pallas_call(
        _inproj_kernel,
        grid=(n // tm,),
        in_specs=[pl.BlockSpec((tm, d), lambda i: (i, 0)), _resident((1, d)), _resident(w.shape)],
        out_specs=[pl.BlockSpec((1, dil, group_tiles, tm // dil, LANES), tile_map) for dil in dils]
                  + [pl.BlockSpec((1, moba_tiles, tm, LANES), lambda i: (i // tiles_per_seq, 0, i % tiles_per_seq, 0)),
                     pl.BlockSpec((tm, gate_w), lambda i: (i, 0))],
        out_shape=[jax.ShapeDtypeStruct((batch, dil, group_tiles, seq // dil, LANES), BF16) for dil in dils]
                  + [jax.ShapeDtypeStruct((batch, moba_tiles, seq, LANES), BF16),
                     jax.ShapeDtypeStruct((n, gate_w), BF16)],
        compiler_params=pltpu.CompilerParams(dimension_semantics=("parallel",),
                                             vmem_limit_bytes=VMEM_LIMIT),
        name="inproj",
    )(x, g, w)


def _dil_kernel(slopes_ref, *refs):
    n_groups = len(DIL_GROUPS)
    qkv_refs = [refs[3 * g:3 * g + 3] for g in range(n_groups)]
    out_refs = [refs[3 * n_groups + 2 * g:3 * n_groups + 2 * g + 2] for g in range(n_groups)]
    qm_ref, vt_ref, bias_ref, ot_ref, lt_ref, nat_ref = refs[5 * n_groups:]
    jp = pl.program_id(1)
    blk = DIL_STEPS

    kk = lax.broadcasted_iota(jnp.int32, (2 * blk, 2 * blk), 0)
    col = lax.broadcasted_iota(jnp.int32, (2 * blk, 2 * blk), 1)
    delta = blk + (col & (blk - 1)) - kk
    window = (delta >= 0) & (delta <= DIL_STEPS)

    units = []
    for g, (_, dilation) in enumerate(DIL_GROUPS):
        q_ref, k_ref, v_ref = qkv_refs[g]
        n_res, sub_len = q_ref.shape[1], q_ref.shape[3]
        head0 = g * DIL_HEADS_PER_GROUP
        slope = jnp.where(col < blk, slopes_ref[head0 + HEADS_PER_TILE * jp],
                          slopes_ref[head0 + HEADS_PER_TILE * jp + 1]) * (float(dilation) * LOG2E)
        bias_ref[g] = jnp.where(window, -slope * delta.astype(F32), MASKED)

        lane = lax.broadcasted_iota(jnp.int32, (sub_len, LANES), 1)
        for r in range(n_res):
            q2 = q_ref[0, r, 0]
            rows = slice(r * sub_len, (r + 1) * sub_len)
            for hh in range(HEADS_PER_TILE):
                in_half = (lane >= hh * HEAD_DIM) & (lane < (hh + 1) * HEAD_DIM)
                qm_ref[g, hh, rows, :] = jnp.where(in_half, q2, jnp.zeros_like(q2))
            vt_ref[g, :LANES, rows] = v_ref[0, r, 0].astype(F32).T.astype(BF16)
        vt_ref[g, LANES:, :] = jnp.ones((ONES_ROWS, n_res * sub_len), BF16)
        units += [(g, r, n) for r in range(n_res) for n in range(sub_len // blk)]

    def rows_of(unit):
        g, r, n = unit
        sub_len = qkv_refs[g][0].shape[3]
        q_rows = slice(n * blk, (n + 1) * blk)
        k_rows = slice((n - 1) * blk, (n + 1) * blk) if n else q_rows
        shift = lambda sl: slice(r * sub_len + sl.start, r * sub_len + sl.stop)
        return q_rows, k_rows, shift(q_rows), shift(k_rows)

    def scores(unit):
        g, r, n = unit
        _, k_rows, q_flat, _ = rows_of(unit)
        queries = jnp.concatenate([qm_ref[g, 0, q_flat, :], qm_ref[g, 1, q_flat, :]], axis=0)
        s = lax.dot_general(qkv_refs[g][1][0, r, 0, k_rows, :], queries, _NT, preferred_element_type=F32)
        s = s + (bias_ref[g] if n else bias_ref[g, blk:, :])
        return s, jnp.max(s, axis=0, keepdims=True)

    def probs(unit, s, m):
        return jnp.exp2(s - m).astype(BF16), m

    def outputs(unit, p, m):
        g = unit[0]
        _, _, q_flat, k_flat = rows_of(unit)
        o = jnp.dot(vt_ref[g, :, k_flat], p, preferred_element_type=F32)
        l = o[LANES:LANES + 1, :]
        inv = 1.0 / l
        lse = m + jnp.log2(l)
        for hh in range(HEADS_PER_TILE):
            feat = slice(hh * HEAD_DIM, (hh + 1) * HEAD_DIM)
            qcol = slice(hh * blk, (hh + 1) * blk)
            ot_ref[g, feat, q_flat] = o[feat, qcol] * inv[:, qcol]
            lt_ref[g, feat, q_flat] = jnp.broadcast_to(lse[:, qcol], (HEAD_DIM, blk))

    _emit_pipelined(units, (scores, probs, outputs), lag=2)

    for g, (_, dilation) in enumerate(DIL_GROUPS):
        o_ref, lse_ref = out_refs[g]
        n_res, sub_len = qkv_refs[g][0].shape[1], qkv_refs[g][0].shape[3]
        for r in range(n_res):
            rows = slice(r * sub_len, (r + 1) * sub_len)
            if dilation == 1:
                o_ref[0, 0] = ot_ref[g, :, rows].T.astype(o_ref.dtype)
                lse_ref[0, 0] = lt_ref[g, :, rows].T
            else:
                nat_ref[pl.ds(r, sub_len, stride=dilation), :] = ot_ref[g, :, rows].T
                lse_ref.at[0, 0][pl.ds(r, sub_len, stride=dilation), :] = lt_ref[g, :, rows].T
        if dilation > 1:
            o_ref[0, 0] = nat_ref[...].astype(o_ref.dtype)


def _dilated_attention(slopes, qkv_groups):
    batch = qkv_groups[0].shape[0]
    seq = qkv_groups[0].shape[1] * qkv_groups[0].shape[3]
    tiles_per_part = DIL_OUT // LANES
    n_groups = len(qkv_groups)

    def spec(arr, part):
        _, n_res, _, sub_len, _ = arr.shape
        return pl.BlockSpec((1, n_res, 1, sub_len, LANES), lambda b, jp: (b, 0, part * tiles_per_part + jp, 0, 0))

    nat_spec = pl.BlockSpec((1, 1, seq, LANES), lambda b, jp: (b, jp, 0, 0))
    out_shape, out_specs = [], []
    for arr in qkv_groups:
        for dtype in (BF16, F32):
            out_shape.append(jax.ShapeDtypeStruct((batch, tiles_per_part, seq, LANES), dtype))
            out_specs.append(nat_spec)
    outs = pl.pallas_call(
        _dil_kernel,
        grid=(batch, tiles_per_part),
        in_specs=[pl.BlockSpec(memory_space=pltpu.SMEM)] + [spec(arr, part) for arr in qkv_groups for part in range(3)],
        out_specs=out_specs,
        out_shape=out_shape,
        scratch_shapes=[pltpu.VMEM((n_groups, HEADS_PER_TILE, seq, LANES), BF16),
                        pltpu.VMEM((n_groups, LANES + ONES_ROWS, seq), BF16),
                        pltpu.VMEM((n_groups, 2 * DIL_STEPS, 2 * DIL_STEPS), F32),
                        pltpu.VMEM((n_groups, LANES, seq), F32),
                        pltpu.VMEM((n_groups, LANES, seq), F32),
                        pltpu.VMEM((seq, LANES), F32)],
        compiler_params=pltpu.CompilerParams(dimension_semantics=("parallel", "parallel"),
                                             vmem_limit_bytes=VMEM_LIMIT),
        name="dilated_attn",
    )(slopes, *[arr for arr in qkv_groups for _ in range(3)])
    return outs[0::2], outs[1::2]


N_MOBA_BLOCKS = 8


def _moba_prepare(jp, slopes_ref, qkv_ref, qa_ref, ka_ref, vt_ref):
    seq = qkv_ref.shape[2]
    n_pairs = MOBA_WIDTH // LANES
    nb = N_MOBA_BLOCKS
    shift = MOBA_BLOCK.bit_length() - 1
    sel_lanes = HEADS_PER_TILE * nb

    q2 = qkv_ref[0, jp]
    k2 = qkv_ref[0, n_pairs + jp]
    v_t = qkv_ref[0, 2 * n_pairs + jp].astype(F32).T.astype(BF16)
    for hh in range(HEADS_PER_TILE):
        vt_ref[jp, hh, :HEAD_DIM, :] = v_t[hh * HEAD_DIM:(hh + 1) * HEAD_DIM, :]
        vt_ref[jp, hh, HEAD_DIM:, :] = jnp.ones((ONES_ROWS, seq), BF16)

    lane = lax.broadcasted_iota(jnp.int32, (seq, LANES), 1)
    row = lax.broadcasted_iota(jnp.int32, (seq, LANES), 0)
    one_hot = jnp.where((lane & (nb - 1)) == (row >> shift), 1.0, 0.0)
    in_block = (row & (MOBA_BLOCK - 1)).astype(F32)
    n_one_hot = sel_lanes + 3 * nb
    k_extra = jnp.where(lane < n_one_hot, one_hot, jnp.where(lane < n_one_hot + 3, in_block, 0.0))
    ka_ref[jp, :, :LANES] = k2
    ka_ref[jp, :, LANES:] = k_extra.astype(BF16)

    a_row = lax.broadcasted_iota(jnp.int32, (LANES, seq), 0)
    a_col = lax.broadcasted_iota(jnp.int32, (LANES, seq), 1)
    avg = jnp.where((a_row < sel_lanes) & ((a_row & (nb - 1)) == (a_col >> shift)),
                    1.0 / MOBA_BLOCK, 0.0).astype(BF16)
    km = jnp.dot(avg, k2, preferred_element_type=F32)
    km_row = lax.broadcasted_iota(jnp.int32, (LANES, LANES), 0)
    km_lane = lax.broadcasted_iota(jnp.int32, (LANES, LANES), 1)
    km = jnp.where((km_row < sel_lanes) & ((km_row // nb) == (km_lane // HEAD_DIM)), km, 0.0)
    gate_t = sum(lax.dot_general(part, q2, _NT, preferred_element_type=F32) for part in _split3(km))

    blk_idx = lax.broadcasted_iota(jnp.int32, (nb, seq), 0)
    own_blk = lax.broadcasted_iota(jnp.int32, (nb, seq), 1) >> shift
    is_past = blk_idx < own_blk
    sels = []
    for hh in range(HEADS_PER_TILE):
        g = jnp.where(is_past, gate_t[hh * nb:(hh + 1) * nb, :], -jnp.inf)
        cnt = jnp.zeros((nb, seq), F32)
        for m in range(nb):
            tie = jnp.where(blk_idx > m, 1.0, 0.0)
            cnt = cnt + jnp.where(g[m:m + 1, :] > g, 1.0, jnp.where(g[m:m + 1, :] == g, tie, 0.0))
        keep = (is_past & (cnt < MOBA_TOPK)) | (blk_idx == own_blk)
        sels.append(jnp.where(keep, 0.0, MASKED))
    sel_t = jnp.concatenate(sels + [jnp.zeros((LANES - sel_lanes, seq), F32)], axis=0)

    q_t = q2.astype(F32).T
    feat = lax.broadcasted_iota(jnp.int32, (LANES, seq), 0)
    lane1 = lax.broadcasted_iota(jnp.int32, (LANES, 1), 0)
    for hh in range(HEADS_PER_TILE):
        slope = slopes_ref[DIL_HEADS + HEADS_PER_TILE * jp + hh] * LOG2E
        base = jnp.where(lane1 < n_one_hot, slope * float(MOBA_BLOCK) * (lane1 & (nb - 1)).astype(F32), slope)
        term = jnp.where(lane1 < n_one_hot, (lane1 - sel_lanes) // nb, lane1 - n_one_hot)
        term = jnp.where(lane1 < sel_lanes, -1, term)
        hi, mid, lo = _split3(base)
        q_const = jnp.where(term == 0, hi, jnp.where(term == 1, mid, jnp.where(term == 2, lo, jnp.zeros_like(hi))))
        own_sel = (feat >= hh * nb) & (feat < (hh + 1) * nb)
        in_head = (feat >= hh * HEAD_DIM) & (feat < (hh + 1) * HEAD_DIM)
        qa_ref[jp, hh, :LANES, :] = jnp.where(in_head, q_t, 0.0).astype(BF16)
        qa_ref[jp, hh, LANES:, :] = jnp.where(own_sel, sel_t.astype(BF16), q_const)


def _moba_kernel(slopes_ref, qkv_ref, *refs, n_casts, scaled_cast):
    n_in = n_casts + (2 if scaled_cast else 0)
    n_out = 1 + n_casts + (1 if scaled_cast else 0)
    ins, outs = refs[:n_in], refs[n_in:n_in + n_out]
    qa_ref, ka_ref, vt_ref, ot_ref = refs[n_in + n_out:]
    o_ref = outs[0]
    for src, dst in zip(ins[:n_casts], outs[1:1 + n_casts]):
        dst[...] = src[...].astype(dst.dtype)
    if scaled_cast:
        w_ref, col_scale_ref = ins[n_casts:]
        outs[-1][...] = (w_ref[...] * col_scale_ref[...]).astype(outs[-1].dtype)
    n_pairs = MOBA_WIDTH // LANES
    nb = N_MOBA_BLOCKS
    for jp in range(n_pairs):
        _moba_prepare(jp, slopes_ref, qkv_ref, qa_ref, ka_ref, vt_ref)

    ki = lax.broadcasted_iota(jnp.int32, (MOBA_BLOCK, HEADS_PER_TILE * MOBA_BLOCK), 0)
    qi = lax.broadcasted_iota(jnp.int32, (MOBA_BLOCK, HEADS_PER_TILE * MOBA_BLOCK), 1) & (MOBA_BLOCK - 1)
    causal = jnp.where(ki <= qi, 0.0, MASKED)

    def scores(unit):
        jp, c = unit
        start, stop = c * MOBA_BLOCK, (c + 1) * MOBA_BLOCK
        queries = jnp.concatenate([qa_ref[jp, 0, :, start:stop], qa_ref[jp, 1, :, start:stop]], axis=1)
        s = jnp.dot(ka_ref[jp, :stop, :], queries, preferred_element_type=F32)
        s_own = s[start:, :] + causal
        m = jnp.max(s_own, axis=0, keepdims=True)
        if c:
            m = jnp.maximum(m, jnp.max(s[:start, :], axis=0, keepdims=True))
        return (s[:start, :] if c else None), s_own, m

    def probs(unit, s_past, s_own, m):
        p_own = jnp.exp2(s_own - m).astype(BF16)
        if not unit[1]:
            return (p_own,)
        return (jnp.concatenate([jnp.exp2(s_past - m).astype(BF16), p_own], axis=0),)

    def outputs(unit, p):
        jp, c = unit
        start, stop = c * MOBA_BLOCK, (c + 1) * MOBA_BLOCK
        for hh in range(HEADS_PER_TILE):
            o = jnp.dot(vt_ref[jp, hh, :, :stop], p[:, hh * MOBA_BLOCK:(hh + 1) * MOBA_BLOCK],
                        preferred_element_type=F32)
            ot_ref[jp, hh * HEAD_DIM:(hh + 1) * HEAD_DIM, start:stop] = o[:HEAD_DIM] * (1.0 / o[HEAD_DIM:HEAD_DIM + 1])

    _emit_pipelined([(jp, c) for jp in range(n_pairs) for c in range(nb)], (scores, probs, outputs), lag=1)
    for jp in range(n_pairs):
        o_ref[0, jp] = ot_ref[jp].T.astype(o_ref.dtype)


def _moba_attention(slopes, qkv, layer, stacked_weights, scaled=None):
    batch, n_tiles, seq, _ = qkv.shape
    assert seq == N_MOBA_BLOCKS * MOBA_BLOCK
    n_pairs = MOBA_WIDTH // LANES
    cast_in, cast_out, cast_shape, extra = [], [], [], []
    casts = [(w, layer) for w in stacked_weights] + ([scaled[:2]] if scaled else [])
    for w, index in casts:
        _, rows, cols = w.shape
        slab = rows // batch
        assert slab * batch == rows and slab % 16 == 0
        cast_in.append(pl.BlockSpec((None, slab, cols), lambda b, index=index: (index, b, 0)))
        cast_out.append(pl.BlockSpec((slab, cols), lambda b: (b, 0)))
        cast_shape.append(jax.ShapeDtypeStruct((rows, cols), BF16))
        extra.append(w)
    if scaled:
        cast_in.append(_resident(scaled[2].shape))
        extra.append(scaled[2])
    return pl.pallas_call(
        functools.partial(_moba_kernel, n_casts=len(stacked_weights), scaled_cast=scaled is not None),
        grid=(batch,),
        in_specs=[pl.BlockSpec(memory_space=pltpu.SMEM),
                  pl.BlockSpec((1, n_tiles, seq, LANES), lambda b: (b, 0, 0, 0))] + cast_in,
        out_specs=[pl.BlockSpec((1, n_pairs, seq, LANES), lambda b: (b, 0, 0, 0))] + cast_out,
        out_shape=[jax.ShapeDtypeStruct((batch, n_pairs, seq, LANES), BF16)] + cast_shape,
        scratch_shapes=[pltpu.VMEM((n_pairs, HEADS_PER_TILE, 2 * LANES, seq), BF16),
                        pltpu.VMEM((n_pairs, seq, 2 * LANES), BF16),
                        pltpu.VMEM((n_pairs, HEADS_PER_TILE, HEAD_DIM + ONES_ROWS, seq), BF16),
                        pltpu.VMEM((n_pairs, LANES, seq), F32)],
        compiler_params=pltpu.CompilerParams(dimension_semantics=("parallel",),
                                             vmem_limit_bytes=VMEM_LIMIT),
        name="moba_attn",
    )(slopes, qkv, *extra)


MIX_SUB_ROWS = 256


def _mix_out_kernel(x_ref, o0_ref, o1_ref, o2_ref, l0_ref, l1_ref, l2_ref, ob_ref, ga_ref, gb_ref,
                    wa_ref, wb_ref, wo_ref, g_ref, out_ref):
    tm = x_ref.shape[0]
    l_refs, o_refs = (l0_ref, l1_ref, l2_ref), (o0_ref, o1_ref, o2_ref)

    def rows(j):
        return slice(j * MIX_SUB_ROWS, (j + 1) * MIX_SUB_ROWS)

    def load(ref, j):
        return jnp.concatenate([ref[0, t, rows(j), :].astype(F32) for t in range(ref.shape[1])], axis=1)

    def merge(j):
        lses = [load(ref, j) for ref in l_refs]
        outs = [load(ref, j) for ref in o_refs]
        top = jnp.maximum(jnp.maximum(lses[0], lses[1]), lses[2])
        es = [jnp.exp2(l - top) for l in lses]
        inv = 1.0 / (es[0] + es[1] + es[2])
        o_a = (es[0] * inv) * outs[0] + (es[1] * inv) * outs[1] + (es[2] * inv) * outs[2]
        return (o_a.astype(BF16),)

    def branches(j, o_a):
        y_a = jnp.dot(o_a, wa_ref[...], preferred_element_type=F32)
        o_b = jnp.concatenate([ob_ref[0, t, rows(j), :] for t in range(ob_ref.shape[1])], axis=1)
        y_b = jnp.dot(o_b, wb_ref[...], preferred_element_type=F32)
        h_a, h_b = 0.5 * y_a, 0.5 * y_b
        t_a = ga_ref[rows(j), :].astype(F32)
        t_b = gb_ref[rows(j), :].astype(F32)
        return (((h_a + h_a * t_a) + (h_b + h_b * t_b)).astype(BF16),)

    def project(j, merged):
        return (jnp.dot(merged, wo_ref[...], preferred_element_type=F32),)

    def finish(j, z):
        out_ref[rows(j), :] = x_ref[rows(j), :] + _rms_norm(z, g_ref[...])

    _emit_pipelined(list(range(tm // MIX_SUB_ROWS)), (merge, branches, project, finish), lag=1)


def _mix_out(x, dil_outs, dil_lses, o_moba, gates, w_a, w_b, w_o, g_post, seq, tm):
    n, d = x.shape
    tiles_per_seq = seq // tm
    row = lambda w, col=0: pl.BlockSpec((tm, w), lambda i: (i, col))

    def group_spec(arr):
        return pl.BlockSpec((1, arr.shape[1], tm, LANES), lambda i: (i // tiles_per_seq, 0, i % tiles_per_seq, 0))

    moba_spec = pl.BlockSpec((1, o_moba.shape[1], tm, LANES), lambda i: (i // tiles_per_seq, 0, i % tiles_per_seq, 0))

    return pl.pallas_call(
        _mix_out_kernel,
        grid=(n // tm,),
        in_specs=[row(d)] + [group_spec(a) for a in dil_outs] + [group_spec(a) for a in dil_lses]
                 + [moba_spec, row(d, 0), row(d, 1),
                    _resident(w_a.shape), _resident(w_b.shape), _resident(w_o.shape), _resident((1, d))],
        out_specs=row(d),
        out_shape=jax.ShapeDtypeStruct((n, d), F32),
        compiler_params=pltpu.CompilerParams(dimension_semantics=("parallel",),
                                             vmem_limit_bytes=VMEM_LIMIT),
        name="mix_out",
    )(x, *dil_outs, *dil_lses, o_moba, gates, gates, w_a, w_b, w_o, g_post)


CARRY_ROWS = 8


def _ffn_kernel(x_ref, gpre_ref, gpost_ref, wg_ref, wu_ref, cw_ref, cb_ref, wd_ref, out_ref, a_ref,
                *, tiles_per_seq):
    tm = x_ref.shape[0]
    x = x_ref[...]
    h = _rms_norm(x, gpre_ref[...]).astype(BF16)

    @pl.when(pl.program_id(0) % tiles_per_seq == 0)
    def _():
        a_ref[0:CARRY_ROWS, :] = jnp.zeros((CARRY_ROWS, a_ref.shape[1]), F32)

    a_ref[CARRY_ROWS:CARRY_ROWS + tm, :] = jnp.dot(h, wg_ref[...], preferred_element_type=F32)
    up = jnp.dot(h, wu_ref[...], preferred_element_type=F32)
    conv = cb_ref[...]
    for tap in range(CONV_WIDTH):
        back = CONV_WIDTH - 1 - tap
        conv = conv + a_ref[CARRY_ROWS - back:CARRY_ROWS - back + tm, :] * cw_ref[tap:tap + 1, :]
    a_ref[0:CARRY_ROWS, :] = a_ref[tm:tm + CARRY_ROWS, :]
    inner = 0.7978845608028654 * (conv + 0.044715 * (conv * conv * conv))
    u = (0.5 * conv * (1.0 + jnp.tanh(inner))) * up
    z = jnp.dot(u.astype(BF16), wd_ref[...], preferred_element_type=F32)
    out_ref[...] = x + _rms_norm(z, gpost_ref[...])


def _ffn(x, g_pre, g_post, w_gate, w_up, conv_w, conv_b, w_down, seq, tm):
    n, d = x.shape
    d_ff = w_gate.shape[1]
    row = pl.BlockSpec((tm, d), lambda i: (i, 0))
    return pl.pallas_call(
        functools.partial(_ffn_kernel, tiles_per_seq=seq // tm),
        grid=(n // tm,),
        in_specs=[row, _resident((1, d)), _resident((1, d)), _resident(w_gate.shape),
                  _resident(w_up.shape), _resident(conv_w.shape), _resident((1, d_ff)),
                  _resident(w_down.shape)],
        out_specs=row,
        out_shape=jax.ShapeDtypeStruct((n, d), F32),
        scratch_shapes=[pltpu.VMEM((tm + CARRY_ROWS, d_ff), F32)],
        compiler_params=pltpu.CompilerParams(dimension_semantics=("arbitrary",),
                                             vmem_limit_bytes=VMEM_LIMIT),
        name="conv_ffn",
    )(x, g_pre, g_post, w_gate, w_up, conv_w, conv_b, w_down)


def kernel(x, mix_norm_pre, mix_norm_post, w_in, w_branch_dil, w_branch_moba, w_out, ffn_norm_pre, ffn_norm_post, w_ffn_gate, w_ffn_up, ffn_conv_w, ffn_conv_b, w_ffn_down):
    batch, seq, d = x.shape
    depth = w_in.shape[0]
    idx = jnp.arange(1, N_ATTN_HEADS + 1, dtype=F32)
    slopes = jnp.exp2(-8.0 * idx / N_ATTN_HEADS)
    qkv_dil_cols = 3 * DIL_WIDTH
    col = jnp.arange(w_in.shape[2])
    is_q = (col < DIL_WIDTH) | ((col >= qkv_dil_cols) & (col < qkv_dil_cols + MOBA_WIDTH))
    q_scale = jnp.where(is_q, SCALE * LOG2E, 1.0).astype(F32)

    xf = x.reshape(batch * seq, d)
    w = (w_in[0] * q_scale).astype(BF16)
    for l in range(depth):
        *qkv_groups, qkv_moba, gates = _inproj(xf, mix_norm_pre[l][None], w, batch, seq, MIXER_TILE_ROWS)
        dil_outs, dil_lses = _dilated_attention(slopes, qkv_groups)
        next_w_in = (w_in, l + 1, q_scale[None]) if l + 1 < depth else None
        o_moba, w_a, w_b, w_o, w_gate, w_up, w_down, *w_next = _moba_attention(
            slopes, qkv_moba, l, (w_branch_dil, w_branch_moba, w_out, w_ffn_gate, w_ffn_up, w_ffn_down),
            scaled=next_w_in)
        if w_next:
            w = w_next[0]
        xf = _mix_out(xf, dil_outs, dil_lses, o_moba, gates, w_a, w_b, w_o,
                      mix_norm_post[l][None], seq, MIXER_TILE_ROWS)
        xf = _ffn(xf, ffn_norm_pre[l][None], ffn_norm_post[l][None], w_gate, w_up,
                  ffn_conv_w[l], ffn_conv_b[l][None], w_down, seq, FFN_TILE_ROWS)
    return xf.reshape(batch, seq, d)
```
